```python
import jax, jax.numpy as jnp
from jax import lax
import numpy as np

D_MODEL = 1024
BATCH = 8
SEQ = 4096
DEPTH = 1

CHUNK = 64
EPS = 1e-6

SSD_WIDTH = D_MODEL
SSD_HEAD_DIM = 64
SSD_HEADS = SSD_WIDTH // SSD_HEAD_DIM
SSD_GROUPS = 2
SSD_STATE = 128
SSD_CONV = 4
SSD_XBC = SSD_WIDTH + 2 * SSD_GROUPS * SSD_STATE

SC_WIDTH = D_MODEL
SC_GROUPS = 16
SC_CONV = 3

MIX_WIDTH = SSD_WIDTH + SC_WIDTH
IN_SIZES = (SSD_WIDTH, SSD_XBC, SSD_HEADS, SC_WIDTH, SC_WIDTH, SC_WIDTH)
IN_COLS = SSD_WIDTH + SSD_XBC + SSD_HEADS + 3 * SC_WIDTH

FFN_HIDDEN = 2816
FFN_CONV = 3

DT_MIN = 0.001
DT_MAX = 0.1

kernel_name = "hybrid_ssd_shortconv_adaln_block"


def rmsnorm(x, w):
    xf = x.astype(jnp.float32)
    y = xf * lax.rsqrt(jnp.mean(xf * xf, axis=-1, keepdims=True) + EPS)
    return (y * w.astype(jnp.float32)).astype(x.dtype)


def group_rmsnorm(x, w, groups):
    b, l, ch = x.shape
    xf = x.astype(jnp.float32).reshape(b, l, groups, ch // groups)
    y = xf * lax.rsqrt(jnp.mean(xf * xf, axis=-1, keepdims=True) + EPS)
    return (y.reshape(b, l, ch) * w.astype(jnp.float32)).astype(x.dtype)


def causal_dwconv(x, w):
    k = w.shape[0]
    return lax.conv_general_dilated(
        x, w[:, None, :].astype(x.dtype), window_strides=(1,), padding=[(k - 1, 0)],
        dimension_numbers=('NWC', 'WIO', 'NWC'), feature_group_count=x.shape[-1])


def ssd_scan(xh, dt, a_neg, bm, cm):
    out_dtype = xh.dtype
    b, seq, nh, p = xh.shape
    g, n = bm.shape[-2:]
    r = nh // g
    nc = seq // CHUNK
    xh = xh.astype(jnp.float32)
    dt = dt.astype(jnp.float32)
    xdt = (xh * dt[..., None]).reshape(b, nc, CHUNK, g, r, p)
    a = (dt * a_neg.astype(jnp.float32)).reshape(b, nc, CHUNK, g, r)
    a = jnp.moveaxis(a, 2, -1)
    bc = bm.astype(jnp.float32).reshape(b, nc, CHUNK, g, n)
    cc = cm.astype(jnp.float32).reshape(b, nc, CHUNK, g, n)
    a_cs = jnp.cumsum(a, axis=-1)
    seg = a_cs[..., :, None] - a_cs[..., None, :]
    tri = jnp.tril(jnp.ones((CHUNK, CHUNK), dtype=bool))
    decay_in = jnp.exp(jnp.where(tri, seg, -jnp.inf))
    cb = jnp.einsum('bclgn,bcsgn->bcgls', cc, bc)
    y_diag = jnp.einsum('bcgls,bcgrls,bcsgrp->bclgrp', cb, decay_in, xdt)
    decay_to_end = jnp.exp(a_cs[..., -1:] - a_cs)
    chunk_states = jnp.einsum('bcsgn,bcgrs,bcsgrp->bcgrpn', bc, decay_to_end, xdt)
    chunk_decay = jnp.exp(a_cs[..., -1])

    def step(state, inp):
        s_c, d_c = inp
        return state * d_c[..., None, None] + s_c, state

    init = jnp.zeros((b, g, r, p, n), jnp.float32)
    _, prev = lax.scan(step, init, (jnp.moveaxis(chunk_states, 1, 0), jnp.moveaxis(chunk_decay, 1, 0)))
    prev = jnp.moveaxis(prev, 0, 1)
    y_off = jnp.einsum('bclgn,bcgrpn,bcgrl->bclgrp', cc, prev, jnp.exp(a_cs))
    y = (y_diag + y_off).reshape(b, seq, nh, p)
    return y.astype(out_dtype)


def hybrid_mixer(h, w_in, ssd_conv_w, ssd_conv_b, dt_bias, a_log, d_skip, ssd_norm_w,
                 sc_conv_w, sc_norm_w, w_out):
    b, seq, _ = h.shape
    proj = jnp.einsum('bld,de->ble', h, w_in)
    idx, acc = [], 0
    for s in IN_SIZES[:-1]:
        acc += s
        idx.append(acc)
    z, xbc, dt, sc_b, sc_c, sc_h = jnp.split(proj, idx, axis=-1)

    xbc = jax.nn.silu(causal_dwconv(xbc, ssd_conv_w) + ssd_conv_b)
    xs, bm, cm = jnp.split(xbc, [SSD_WIDTH, SSD_WIDTH + SSD_GROUPS * SSD_STATE], axis=-1)
    xs = xs.reshape(b, seq, SSD_HEADS, SSD_HEAD_DIM)
    bm = bm.reshape(b, seq, SSD_GROUPS, SSD_STATE)
    cm = cm.reshape(b, seq, SSD_GROUPS, SSD_STATE)
    dt = jax.nn.softplus(dt + dt_bias)
    a_neg = -jnp.exp(a_log)
    y_ssd = ssd_scan(xs, dt, a_neg, bm, cm) + d_skip[:, None] * xs
    y_ssd = y_ssd.reshape(b, seq, SSD_WIDTH) * jax.nn.silu(z)
    y_ssd = group_rmsnorm(y_ssd, ssd_norm_w, SSD_GROUPS)

    y_sc = sc_b * causal_dwconv(sc_c * sc_h, sc_conv_w)
    y_sc = group_rmsnorm(y_sc, sc_norm_w, SC_GROUPS)

    y = jnp.concatenate([y_ssd, y_sc], axis=-1)
    return jnp.einsum('ble,ed->bld', y, w_out)


def conv_gated_mlp(h, w_up, ffn_conv_w, ffn_conv_b, w_down):
    u = jnp.einsum('bld,df->blf', h, w_up)
    u = causal_dwconv(u, ffn_conv_w) + ffn_conv_b
    g, v = jnp.split(u, 2, axis=-1)
    return jnp.einsum('blf,fd->bld', jax.nn.silu(g) * v, w_down)


def setup_inputs(seed: int = 0) -> dict:
    key = jax.random.key(seed)
    ks = jax.random.split(key, 24)
    f32 = jnp.float32
    nrm = lambda k, shape, s: jax.random.normal(k, shape, f32) * s
    x = jax.random.normal(ks[0], (BATCH, SEQ, D_MODEL), f32)
    c = jax.random.normal(ks[1], (BATCH, D_MODEL), f32)
    w_ada = nrm(ks[2], (DEPTH, D_MODEL, 6 * D_MODEL), 0.5 * D_MODEL ** -0.5)
    b_ada = nrm(ks[3], (DEPTH, 6 * D_MODEL), 0.02)
    norm1_w = 1.0 + nrm(ks[4], (DEPTH, D_MODEL), 0.02)
    w_in = nrm(ks[5], (DEPTH, D_MODEL, IN_COLS), D_MODEL ** -0.5)
    ssd_conv_w = nrm(ks[6], (DEPTH, SSD_CONV, SSD_XBC), SSD_CONV ** -0.5)
    ssd_conv_b = nrm(ks[7], (DEPTH, SSD_XBC), 0.01)
    u = jax.random.uniform(ks[8], (DEPTH, SSD_HEADS), f32)
    dt0 = jnp.exp(u * (np.log(DT_MAX) - np.log(DT_MIN)) + np.log(DT_MIN))
    dt_bias = dt0 + jnp.log(-jnp.expm1(-dt0))
    a_log = jnp.log(jax.random.uniform(ks[9], (DEPTH, SSD_HEADS), f32, 1.0, 16.0))
    d_skip = 1.0 + nrm(ks[10], (DEPTH, SSD_HEADS), 0.02)
    ssd_norm_w = 1.0 + nrm(ks[11], (DEPTH, SSD_WIDTH), 0.02)
    sc_conv_w = nrm(ks[12], (DEPTH, SC_CONV, SC_WIDTH), SC_CONV ** -0.5)
    sc_norm_w = 1.0 + nrm(ks[13], (DEPTH, SC_WIDTH), 0.02)
    w_out = nrm(ks[14], (DEPTH, MIX_WIDTH, D_MODEL), MIX_WIDTH ** -0.5)
    norm2_w = 1.0 + nrm(ks[15], (DEPTH, D_MODEL), 0.02)
    w_up = nrm(ks[16], (DEPTH, D_MODEL, 2 * FFN_HIDDEN), D_MODEL ** -0.5)
    ffn_conv_w = nrm(ks[17], (DEPTH, FFN_CONV, 2 * FFN_HIDDEN), FFN_CONV ** -0.5)
    ffn_conv_b = nrm(ks[18], (DEPTH, 2 * FFN_HIDDEN), 0.01)
    w_down = nrm(ks[19], (DEPTH, FFN_HIDDEN, D_MODEL), FFN_HIDDEN ** -0.5)
    final_norm_w = 1.0 + nrm(ks[20], (D_MODEL,), 0.02)
    return {"x": x, "c": c, "w_ada": w_ada, "b_ada": b_ada, "norm1_w": norm1_w,
            "w_in": w_in, "ssd_conv_w": ssd_conv_w, "ssd_conv_b": ssd_conv_b,
            "dt_bias": dt_bias, "a_log": a_log, "d_skip": d_skip, "ssd_norm_w": ssd_norm_w,
            "sc_conv_w": sc_conv_w, "sc_norm_w": sc_norm_w, "w_out": w_out,
            "norm2_w": norm2_w, "w_up": w_up, "ffn_conv_w": ffn_conv_w,
            "ffn_conv_b": ffn_conv_b, "w_down": w_down, "final_norm_w": final_norm_w}


def reference(x, c, w_ada, b_ada, norm1_w, w_in, ssd_conv_w, ssd_conv_b, dt_bias, a_log,
              d_skip, ssd_norm_w, sc_conv_w, sc_norm_w, w_out, norm2_w, w_up, ffn_conv_w,
              ffn_conv_b, w_down, final_norm_w):
    c_act = jax.nn.silu(c)
    for i in range(DEPTH):
        mod = jnp.einsum('bd,de->be', c_act, w_ada[i]) + b_ada[i]
        sh1, sc1, g1, sh2, sc2, g2 = [m[:, None, :] for m in jnp.split(mod, 6, axis=-1)]
        h = rmsnorm(x, norm1_w[i]) * (1.0 + sc1) + sh1
        x = x + g1 * hybrid_mixer(h, w_in[i], ssd_conv_w[i], ssd_conv_b[i], dt_bias[i],
                                  a_log[i], d_skip[i], ssd_norm_w[i], sc_conv_w[i],
                                  sc_norm_w[i], w_out[i])
        h = rmsnorm(x, norm2_w[i]) * (1.0 + sc2) + sh2
        x = x + g2 * conv_gated_mlp(h, w_up[i], ffn_conv_w[i], ffn_conv_b[i], w_down[i])
    return rmsnorm(x, final_norm_w)
```

```python
import functools

import jax
import jax.numpy as jnp
from jax import lax
from jax.experimental import pallas as pl
from jax.experimental.pallas import tpu as pltpu

EPS = 1e-6
CHUNK = 64
HEAD_DIM = 64
SSD_GROUPS = 2
SSD_STATE = 128
SC_GROUP = 64
LANES = 128
SUBLANES = 8
HALO = SUBLANES

MIX_TILE = 256
FFN_TILE = 512
FFN_COLS = 256

VMEM_LIMIT_BYTES = 56 * 1024 * 1024

BF16 = jnp.bfloat16
F32 = jnp.float32
NEG_BIG = -1e30


def _dot(a, b):
    return jnp.dot(a, b, preferred_element_type=F32)


def _sigmoid(v):
    return 1.0 / (1.0 + jnp.exp(-v))


def _silu(v):
    return v * _sigmoid(v)


def _split_bf16(v):
    hi = v.astype(BF16)
    lo = (v - hi.astype(F32)).astype(BF16)
    return jnp.concatenate([hi, lo], axis=1)


def _resident(shape):
    nd = len(shape)
    return pl.BlockSpec(shape, lambda *_: (0,) * nd, pipeline_mode=pl.Buffered(1))


def _mod_kernel(c_ref, w_ref, b_ref, o_ref):
    c = c_ref[...]
    o_ref[...] = _dot(_silu(c).astype(BF16), w_ref[...].astype(BF16)) + b_ref[...]


def _adaln_mod(c, w_ada, b_ada):
    bsz, d = c.shape
    n = w_ada.shape[1]
    return pl.pallas_call(
        _mod_kernel,
        out_shape=jax.ShapeDtypeStruct((bsz, n), F32),
        grid=(n // d,),
        in_specs=[
            pl.BlockSpec((bsz, d), lambda j: (0, 0)),
            pl.BlockSpec((d, d), lambda j: (0, j)),
            pl.BlockSpec((1, d), lambda j: (0, j)),
        ],
        out_specs=pl.BlockSpec((bsz, d), lambda j: (0, j)),
        compiler_params=pltpu.CompilerParams(dimension_semantics=("arbitrary",)),
        name="adaln_mod",
    )(c, w_ada, b_ada.reshape(1, n))


def _mixer_kernel(x_ref, mod_ref, n1w_ref, wz_ref, wxbc_ref, wdt_ref, wscb_ref, wscc_ref,
                  wsch_ref, cw_ref, cb_ref, dtb_ref, alog_ref, dskip_ref, ssdnw_ref, sccw_ref,
                  scnw_ref, wout_ref, btri_ref, expand_ref, gsum_ref,
                  o_ref,
                  xbuf, pbuf, state, aexp_s, xdt_s, xs_s, bc_s, y_s, gate_s, row_s):
    tl = x_ref.shape[1]
    width = x_ref.shape[2]
    n_chunks = tl // CHUNK
    gcols = width // SSD_GROUPS
    pairs_per_group = gcols // LANES

    @pl.when(pl.program_id(1) == 0)
    def _():
        xbuf[0:HALO, :] = jnp.zeros((HALO, xbuf.shape[1]), F32)
        pbuf[0:HALO, :] = jnp.zeros((HALO, pbuf.shape[1]), F32)
        state[...] = jnp.zeros(state.shape, F32)

    x = x_ref[0]
    sh1 = mod_ref[0, 0:1, :]
    sc1 = mod_ref[0, 1:2, :]
    g1 = mod_ref[0, 2:3, :]
    ms = jnp.mean(x * x, axis=-1, keepdims=True)
    h = (x * lax.rsqrt(ms + EPS)) * (n1w_ref[...] * (1.0 + sc1)) + sh1
    hb = h.astype(BF16)

    gate_s[...] = _silu(_dot(hb, wz_ref[...]))

    xbuf[HALO:HALO + tl, :] = _dot(hb, wxbc_ref[...])
    cw = cw_ref[...]
    acc = (cb_ref[...]
           + cw[3:4, :] * xbuf[HALO:HALO + tl, :]
           + cw[2:3, :] * xbuf[HALO - 1:HALO - 1 + tl, :]
           + cw[1:2, :] * xbuf[HALO - 2:HALO - 2 + tl, :]
           + cw[0:1, :] * xbuf[HALO - 3:HALO - 3 + tl, :])
    xbuf[0:HALO, :] = xbuf[tl:tl + HALO, :]
    xbc = _silu(acc)
    xs = xbc[:, 0:width]
    xs_s[...] = xs
    bc_s[...] = xbc[:, width:]

    dtv = _dot(hb, wdt_ref[...]) + dtb_ref[...]
    dt = jnp.maximum(dtv, 0.0) + jnp.log1p(jnp.exp(-jnp.abs(dtv)))
    a = dt * (-jnp.exp(alog_ref[...]))
    acs2 = _dot(btri_ref[...], _split_bf16(a))
    a_cs = acs2[:, 0:LANES] + acs2[:, LANES:]

    expand = expand_ref[...]
    aexp_s[...] = _dot(_split_bf16(a_cs), expand)
    xdt_s[...] = xs * _dot(_split_bf16(dt), expand)

    acs_t = a_cs.T
    acs_t_rot = pltpu.roll(acs_t, CHUNK, axis=1)
    lane16 = lax.broadcasted_iota(jnp.int32, (2 * SUBLANES, LANES), 1)
    blocks = tl // LANES
    for c in range(n_chunks):
        if c % 2 == 0:
            left = acs_t[0:16, (c // 2) * LANES:(c // 2 + 1) * LANES]
            right = acs_t_rot[1:17, (c // 2) * LANES:(c // 2 + 1) * LANES]
        else:
            nb = ((c + 1) // 2) % blocks
            left = acs_t_rot[0:16, nb * LANES:(nb + 1) * LANES]
            right = acs_t[1:17, (c // 2) * LANES:(c // 2 + 1) * LANES]
        row_s[c] = jnp.where(lane16 < CHUNK, left, right)

    lane = lax.broadcasted_iota(jnp.int32, (CHUNK, LANES), 1)
    srow = lax.broadcasted_iota(jnp.int32, (CHUNK, LANES), 0)
    tri2 = srow >= (lane & (CHUNK - 1))
    left_half = lane < CHUNK

    def chunk_body(c, carry):
        r0 = pl.multiple_of(c * CHUNK, CHUNK)
        rows = pl.ds(r0, CHUNK)
        aexp_c = aexp_s[rows, :]
        last = aexp_s[pl.ds(r0 + CHUNK - 1, 1), :]
        dec_start = jnp.exp(aexp_c)
        dec_end = jnp.exp(last - aexp_c)
        chunk_decay = jnp.exp(last)
        xdt_c = xdt_s[rows, :]
        xw_b = (xdt_c * dec_end).astype(BF16)
        bm_b = bc_s[rows, 0:SSD_GROUPS * SSD_STATE].astype(BF16)
        cm_b = bc_s[rows, SSD_GROUPS * SSD_STATE:].astype(BF16)
        prev = state[...]
        prev_b = prev.astype(BF16)
        for g in range(SSD_GROUPS):
            gs = slice(g * gcols, (g + 1) * gcols)
            cmg = cm_b[:, g * SSD_STATE:(g + 1) * SSD_STATE]
            bmg = bm_b[:, g * SSD_STATE:(g + 1) * SSD_STATE]
            cb2 = lax.dot_general(cmg, jnp.concatenate([bmg, bmg], axis=0),
                                  (((1,), (1,)), ((), ())), preferred_element_type=F32)
            y_off = _dot(cmg, prev_b[:, gs])
            s_new = lax.dot_general(bmg, xw_b[:, gs], (((0,), (0,)), ((), ())),
                                    preferred_element_type=F32)
            state[:, gs] = prev[:, gs] * chunk_decay[:, gs] + s_new
            for hp in range(pairs_per_group):
                p = g * pairs_per_group + hp
                cs = slice(p * LANES, (p + 1) * LANES)
                seg = aexp_c[:, cs] - row_s[c, 2 * p:2 * p + 1, :]
                m2 = (jnp.exp(jnp.where(tri2, seg, NEG_BIG)) * cb2).astype(BF16)
                xd2 = xdt_c[:, cs]
                rhs = jnp.concatenate([jnp.where(left_half, xd2, 0.0),
                                       jnp.where(left_half, 0.0, xd2)], axis=0).astype(BF16)
                y_diag = _dot(m2, rhs)
                y_s[rows, cs] = y_diag + y_off[:, hp * LANES:(hp + 1) * LANES] * dec_start[:, cs]
        return carry

    lax.fori_loop(0, n_chunks, chunk_body, 0)

    y = (y_s[...] + dskip_ref[...] * xs_s[...]) * gate_s[...]
    parts = []
    for g in range(SSD_GROUPS):
        yg = y[:, g * gcols:(g + 1) * gcols]
        inv = lax.rsqrt(jnp.mean(yg * yg, axis=-1, keepdims=True) + EPS)
        parts.append(yg * inv)
    y_ssd = (jnp.concatenate(parts, axis=1) * ssdnw_ref[...]).astype(BF16)

    pbuf[HALO:HALO + tl, :] = _dot(hb, wscc_ref[...]) * _dot(hb, wsch_ref[...])
    sw = sccw_ref[...]
    conv = (sw[2:3, :] * pbuf[HALO:HALO + tl, :]
            + sw[1:2, :] * pbuf[HALO - 1:HALO - 1 + tl, :]
            + sw[0:1, :] * pbuf[HALO - 2:HALO - 2 + tl, :])
    pbuf[0:HALO, :] = pbuf[tl:tl + HALO, :]
    y_sc = _dot(hb, wscb_ref[...]) * conv
    ssq = _dot((y_sc * y_sc).astype(BF16), gsum_ref[...])
    inv_g = lax.rsqrt(ssq * (1.0 / SC_GROUP) + EPS)
    y_scn = (y_sc * _dot(_split_bf16(inv_g), expand) * scnw_ref[...]).astype(BF16)

    mix = _dot(y_ssd, wout_ref[0:width, :]) + _dot(y_scn, wout_ref[width:, :])
    o_ref[0] = x + g1 * mix


def _mixer(x, mod, n1w, wz, wxbc, wdt, wscb, wscc, wsch, cw, cb, dtb, alog, dskip, ssdnw, sccw,
           scnw, wout, btri, expand, gsum):
    bsz, seq, d = x.shape
    tl = MIX_TILE
    xbc_w = wxbc.shape[1]
    consts = (n1w, wz, wxbc, wdt, wscb, wscc, wsch, cw, cb, dtb, alog, dskip, ssdnw, sccw, scnw,
              wout, btri, expand, gsum)
    return pl.pallas_call(
        _mixer_kernel,
        out_shape=jax.ShapeDtypeStruct((bsz, seq, d), F32),
        grid=(bsz, seq // tl),
        in_specs=[
            pl.BlockSpec((1, tl, d), lambda b, l: (b, l, 0)),
            pl.BlockSpec((1, mod.shape[1], d), lambda b, l: (b, 0, 0)),
        ] + [_resident(a.shape) for a in consts],
        out_specs=pl.BlockSpec((1, tl, d), lambda b, l: (b, l, 0)),
        scratch_shapes=[
            pltpu.VMEM((tl + HALO, xbc_w), F32),
            pltpu.VMEM((tl + HALO, d), F32),
            pltpu.VMEM((SSD_STATE, d), F32),
            pltpu.VMEM((tl, d), F32),
            pltpu.VMEM((tl, d), F32),
            pltpu.VMEM((tl, d), F32),
            pltpu.VMEM((tl, xbc_w - d), F32),
            pltpu.VMEM((tl, d), F32),
            pltpu.VMEM((tl, d), F32),
            pltpu.VMEM((tl // CHUNK, 2 * SUBLANES, LANES), F32),
        ],
        compiler_params=pltpu.CompilerParams(
            dimension_semantics=("arbitrary", "arbitrary"),
            vmem_limit_bytes=VMEM_LIMIT_BYTES),
        name="mixer",
    )(x, mod, *consts)


def _ffn_kernel(x_ref, mod_ref, n2w_ref, wup_ref, fcw_ref, fcb_ref, wdown_ref, fnw_ref,
                o_ref, ubuf, carry, act_s):
    tl = x_ref.shape[1]
    cols = 2 * FFN_COLS
    n_col_chunks = wup_ref.shape[1] // cols

    @pl.when(pl.program_id(1) == 0)
    def _():
        carry[...] = jnp.zeros(carry.shape, F32)

    x = x_ref[0]
    sh2 = mod_ref[0, 3:4, :]
    sc2 = mod_ref[0, 4:5, :]
    g2 = mod_ref[0, 5:6, :]
    ms = jnp.mean(x * x, axis=-1, keepdims=True)
    h = (x * lax.rsqrt(ms + EPS)) * (n2w_ref[...] * (1.0 + sc2)) + sh2
    hb = h.astype(BF16)

    for j in range(n_col_chunks):
        cs = slice(j * cols, (j + 1) * cols)
        ub = ubuf.at[j % 2]
        ub[0:HALO, :] = carry[:, cs]
        ub[HALO:HALO + tl, :] = _dot(hb, wup_ref[:, cs])
        w = fcw_ref[:, cs]
        u = (fcb_ref[:, cs]
             + w[2:3, :] * ub[HALO:HALO + tl, :]
             + w[1:2, :] * ub[HALO - 1:HALO - 1 + tl, :]
             + w[0:1, :] * ub[HALO - 2:HALO - 2 + tl, :])
        carry[:, cs] = ub[tl:tl + HALO, :]
        act_s[:, j * FFN_COLS:(j + 1) * FFN_COLS] = (
            _silu(u[:, 0:FFN_COLS]) * u[:, FFN_COLS:]).astype(BF16)

    x2 = x + g2 * _dot(act_s[...], wdown_ref[...])
    ms2 = jnp.mean(x2 * x2, axis=-1, keepdims=True)
    o_ref[0] = (x2 * lax.rsqrt(ms2 + EPS)) * fnw_ref[...]


def _ffn(x, mod, n2w, wup, fcw, fcb, wdown, fnw):
    bsz, seq, d = x.shape
    tl = FFN_TILE
    hidden = wdown.shape[0]
    consts = (n2w, wup, fcw, fcb, wdown, fnw)
    return pl.pallas_call(
        _ffn_kernel,
        out_shape=jax.ShapeDtypeStruct((bsz, seq, d), F32),
        grid=(bsz, seq // tl),
        in_specs=[
            pl.BlockSpec((1, tl, d), lambda b, l: (b, l, 0)),
            pl.BlockSpec((1, mod.shape[1], d), lambda b, l: (b, 0, 0)),
        ] + [_resident(a.shape) for a in consts],
        out_specs=pl.BlockSpec((1, tl, d), lambda b, l: (b, l, 0)),
        scratch_shapes=[
            pltpu.VMEM((2, tl + HALO, 2 * FFN_COLS), F32),
            pltpu.VMEM((HALO, 2 * hidden), F32),
            pltpu.VMEM((tl, hidden), BF16),
        ],
        compiler_params=pltpu.CompilerParams(
            dimension_semantics=("arbitrary", "arbitrary"),
            vmem_limit_bytes=VMEM_LIMIT_BYTES),
        name="ffn",
    )(x, mod, *consts)


def _interleave_gate_value(a, hidden):
    lead = a.shape[:-1]
    a = a.reshape(lead + (2, hidden // FFN_COLS, FFN_COLS))
    a = jnp.swapaxes(a, -3, -2)
    return a.reshape(lead + (2 * hidden,))


def kernel(x, c, w_ada, b_ada, norm1_w, w_in, ssd_conv_w, ssd_conv_b, dt_bias, a_log, d_skip,
           ssd_norm_w, sc_conv_w, sc_norm_w, w_out, norm2_w, w_up, ffn_conv_w, ffn_conv_b, w_down,
           final_norm_w):
    bsz, seq, d = x.shape
    depth = w_in.shape[0]
    heads = dt_bias.shape[1]
    xbc_w = ssd_conv_w.shape[2]
    hidden = w_down.shape[1]
    assert d % LANES == 0 and seq % FFN_TILE == 0 and seq % MIX_TILE == 0
    assert heads * HEAD_DIM == d and heads <= 2 * SUBLANES and hidden % FFN_COLS == 0
    assert xbc_w == d + 2 * SSD_GROUPS * SSD_STATE

    t = jnp.arange(MIX_TILE)
    btri = ((t[:, None] >= t[None, :]) & (t[:, None] // CHUNK == t[None, :] // CHUNK)).astype(BF16)
    ch = jnp.arange(d)
    j2 = jnp.arange(2 * LANES)
    expand = ((j2[:, None] % LANES) == (ch[None, :] // HEAD_DIM)).astype(BF16)
    gsum = ((ch[:, None] // SC_GROUP) == jnp.arange(LANES)[None, :]).astype(BF16)

    def pad_lanes(v):
        return jnp.pad(v, ((0, 0), (0, LANES - v.shape[1])))

    c_act_in = c
    for i in range(depth):
        mod = _adaln_mod(c_act_in, w_ada[i], b_ada[i]).reshape(bsz, 6, d)
        wi = w_in[i].astype(BF16)
        o = 0
        wz = wi[:, o:o + d]; o += d
        wxbc = wi[:, o:o + xbc_w]; o += xbc_w
        wdt = pad_lanes(wi[:, o:o + heads]); o += heads
        wscb = wi[:, o:o + d]; o += d
        wscc = wi[:, o:o + d]; o += d
        wsch = wi[:, o:o + d]
        x = _mixer(
            x, mod, norm1_w[i][None], wz, wxbc, wdt, wscb, wscc, wsch,
            ssd_conv_w[i], ssd_conv_b[i][None], pad_lanes(dt_bias[i][None]),
            pad_lanes(a_log[i][None]), jnp.repeat(d_skip[i], HEAD_DIM)[None],
            ssd_norm_w[i][None], sc_conv_w[i], sc_norm_w[i][None], w_out[i].astype(BF16),
            btri, expand, gsum)
        fnw = final_norm_w[None] if i == depth - 1 else None
        assert fnw is not None, "final norm is fused into the last ffn call"
        x = _ffn(
            x, mod, norm2_w[i][None],
            _interleave_gate_value(w_up[i], hidden).astype(BF16),
            _interleave_gate_value(ffn_conv_w[i], hidden),
            _interleave_gate_value(ffn_conv_b[i][None], hidden),
            w_down[i].astype(BF16), fnw)
    return x
```

```python
import functools

import jax
import jax.numpy as jnp
from jax import lax
from jax.experimental import pallas as pl
from jax.experimental.pallas import tpu as pltpu

EPS = 1e-6
CHUNK = 64
HEAD_DIM = 64
SSD_GROUPS = 2
SSD_STATE = 128
SC_GROUP = 64
LANES = 128
SUBLANES = 8
HALO = SUBLANES

MIX_TILE = 256
FFN_TILE = 512
FFN_COLS = 256
FFN_DOWN_GROUP = 11

VMEM_LIMIT_BYTES = 56 * 1024 * 1024

BF16 = jnp.bfloat16
F32 = jnp.float32
NEG_BIG = -1e30


def _dot(a, b):
    return jnp.dot(a, b, preferred_element_type=F32)


def _sigmoid(v):
    return 1.0 / (1.0 + jnp.exp(-v))


def _silu(v):
    return v * _sigmoid(v)


def _split_bf16(v):
    hi = v.astype(BF16)
    lo = (v - hi.astype(F32)).astype(BF16)
    return jnp.concatenate([hi, lo], axis=1)


def _resident(shape):
    nd = len(shape)
    return pl.BlockSpec(shape, lambda *_: (0,) * nd, pipeline_mode=pl.Buffered(1))


def _mod_kernel(c_ref, w_ref, b_ref, o_ref):
    c = c_ref[...]
    o_ref[...] = _dot(_silu(c).astype(BF16), w_ref[...].astype(BF16)) + b_ref[...]


def _adaln_mod(c, w_ada, b_ada):
    bsz, d = c.shape
    n = w_ada.shape[1]
    return pl.pallas_call(
        _mod_kernel,
        out_shape=jax.ShapeDtypeStruct((bsz, n), F32),
        grid=(n // d,),
        in_specs=[
            pl.BlockSpec((bsz, d), lambda j: (0, 0)),
            pl.BlockSpec((d, d), lambda j: (0, j)),
            pl.BlockSpec((1, d), lambda j: (0, j)),
        ],
        out_specs=pl.BlockSpec((bsz, d), lambda j: (0, j)),
        compiler_params=pltpu.CompilerParams(dimension_semantics=("arbitrary",)),
        name="adaln_mod",
    )(c, w_ada, b_ada.reshape(1, n))


def _mixer_kernel(x_ref, mod_ref, n1w_ref, win_ref, cw_ref, cb_ref, dtb_ref, alog_ref,
                  dskip_ref, ssdnw_ref, sccw_ref,
                  scnw_ref, wout_ref, btri_ref, expand_ref, gsum_ref,
                  o_ref,
                  xbuf, pbuf, state, aexp_s, xdt_s, xs_s, bc_s, y_s, gate_s):
    tl = x_ref.shape[1]
    width = x_ref.shape[2]
    n_chunks = tl // CHUNK
    gcols = width // SSD_GROUPS
    pairs_per_group = gcols // LANES

    @pl.when(pl.program_id(1) == 0)
    def _():
        xbuf[0:HALO, :] = jnp.zeros((HALO, xbuf.shape[1]), F32)
        pbuf[0:HALO, :] = jnp.zeros((HALO, pbuf.shape[1]), F32)
        state[...] = jnp.zeros(state.shape, F32)

    x = x_ref[0]
    sh1 = mod_ref[0, 0:1, :]
    sc1 = mod_ref[0, 1:2, :]
    g1 = mod_ref[0, 2:3, :]
    ms = jnp.mean(x * x, axis=-1, keepdims=True)
    h = (x * lax.rsqrt(ms + EPS)) * (n1w_ref[...] * (1.0 + sc1)) + sh1
    hb = h.astype(BF16)

    xbc_w = xbuf.shape[1]
    col_z, col_xbc = 0, width
    col_scb = col_xbc + xbc_w
    col_scc, col_sch, col_dt = col_scb + width, col_scb + 2 * width, col_scb + 3 * width

    def in_proj(col0, ncols):
        return _dot(hb, win_ref[:, col0:col0 + ncols])

    gate_s[...] = _silu(in_proj(col_z, width))

    xbuf[HALO:HALO + tl, :] = in_proj(col_xbc, xbc_w)
    cw = cw_ref[...]
    acc = (cb_ref[...]
           + cw[3:4, :] * xbuf[HALO:HALO + tl, :]
           + cw[2:3, :] * xbuf[HALO - 1:HALO - 1 + tl, :]
           + cw[1:2, :] * xbuf[HALO - 2:HALO - 2 + tl, :]
           + cw[0:1, :] * xbuf[HALO - 3:HALO - 3 + tl, :])
    xbuf[0:HALO, :] = xbuf[tl:tl + HALO, :]
    xbc = _silu(acc)
    xs = xbc[:, 0:width]
    xs_s[...] = xs
    bc_s[...] = xbc[:, width:]

    dtv = in_proj(col_dt, LANES) + dtb_ref[...]
    dt = jnp.maximum(dtv, 0.0) + jnp.log1p(jnp.exp(-jnp.abs(dtv)))
    a = dt * (-jnp.exp(alog_ref[...]))
    acs2 = _dot(btri_ref[...], _split_bf16(a))
    a_cs = acs2[:, 0:LANES] + acs2[:, LANES:]

    expand = expand_ref[...]
    aexp_s[...] = _dot(_split_bf16(a_cs), expand)
    xdt_s[...] = xs * _dot(_split_bf16(dt), expand)

    acs_t = a_cs.T
    acs_t_rot = pltpu.roll(acs_t, CHUNK, axis=1)
    lane16 = lax.broadcasted_iota(jnp.int32, (2 * SUBLANES, LANES), 1)
    blocks = tl // LANES
    chunk_rows = []
    for c in range(n_chunks):
        if c % 2 == 0:
            left = acs_t[0:16, (c // 2) * LANES:(c // 2 + 1) * LANES]
            right = acs_t_rot[1:17, (c // 2) * LANES:(c // 2 + 1) * LANES]
        else:
            nb = ((c + 1) // 2) % blocks
            left = acs_t_rot[0:16, nb * LANES:(nb + 1) * LANES]
            right = acs_t[1:17, (c // 2) * LANES:(c // 2 + 1) * LANES]
        chunk_rows.append(jnp.where(lane16 < CHUNK, left, right))

    lane = lax.broadcasted_iota(jnp.int32, (CHUNK, LANES), 1)
    srow = lax.broadcasted_iota(jnp.int32, (CHUNK, LANES), 0)
    tri2 = srow >= (lane & (CHUNK - 1))
    left_half = lane < CHUNK

    for c in range(n_chunks):
        rows = slice(c * CHUNK, (c + 1) * CHUNK)
        aexp_c = aexp_s[rows, :]
        last = aexp_s[(c + 1) * CHUNK - 1:(c + 1) * CHUNK, :]
        dec_start = jnp.exp(aexp_c)
        dec_end = jnp.exp(last - aexp_c)
        chunk_decay = jnp.exp(last)
        xdt_c = xdt_s[rows, :]
        xw_b = (xdt_c * dec_end).astype(BF16)
        bm_b = bc_s[rows, 0:SSD_GROUPS * SSD_STATE].astype(BF16)
        cm_b = bc_s[rows, SSD_GROUPS * SSD_STATE:].astype(BF16)
        prev = state[...]
        prev_b = prev.astype(BF16)
        for g in range(SSD_GROUPS):
            gs = slice(g * gcols, (g + 1) * gcols)
            cmg = cm_b[:, g * SSD_STATE:(g + 1) * SSD_STATE]
            bmg = bm_b[:, g * SSD_STATE:(g + 1) * SSD_STATE]
            cb2 = lax.dot_general(cmg, jnp.concatenate([bmg, bmg], axis=0),
                                  (((1,), (1,)), ((), ())), preferred_element_type=F32)
            y_off = _dot(cmg, prev_b[:, gs])
            s_new = lax.dot_general(bmg, xw_b[:, gs], (((0,), (0,)), ((), ())),
                                    preferred_element_type=F32)
            state[:, gs] = prev[:, gs] * chunk_decay[:, gs] + s_new
            for hp in range(pairs_per_group):
                p = g * pairs_per_group + hp
                cs = slice(p * LANES, (p + 1) * LANES)
                seg = aexp_c[:, cs] - chunk_rows[c][2 * p:2 * p + 1, :]
                m2 = (jnp.exp(jnp.where(tri2, seg, NEG_BIG)) * cb2).astype(BF16)
                xd2 = xdt_c[:, cs]
                rhs = jnp.concatenate([jnp.where(left_half, xd2, 0.0),
                                       jnp.where(left_half, 0.0, xd2)], axis=0).astype(BF16)
                y_diag = _dot(m2, rhs)
                y_s[rows, cs] = y_diag + y_off[:, hp * LANES:(hp + 1) * LANES] * dec_start[:, cs]

    y = (y_s[...] + dskip_ref[...] * xs_s[...]) * gate_s[...]
    parts = []
    for g in range(SSD_GROUPS):
        yg = y[:, g * gcols:(g + 1) * gcols]
        inv = lax.rsqrt(jnp.mean(yg * yg, axis=-1, keepdims=True) + EPS)
        parts.append(yg * inv)
    y_ssd = (jnp.concatenate(parts, axis=1) * ssdnw_ref[...]).astype(BF16)

    pbuf[HALO:HALO + tl, :] = in_proj(col_scc, width) * in_proj(col_sch, width)
    sw = sccw_ref[...]
    conv = (sw[2:3, :] * pbuf[HALO:HALO + tl, :]
            + sw[1:2, :] * pbuf[HALO - 1:HALO - 1 + tl, :]
            + sw[0:1, :] * pbuf[HALO - 2:HALO - 2 + tl, :])
    pbuf[0:HALO, :] = pbuf[tl:tl + HALO, :]
    y_sc = in_proj(col_scb, width) * conv
    ssq = _dot((y_sc * y_sc).astype(BF16), gsum_ref[...])
    inv_g = lax.rsqrt(ssq * (1.0 / SC_GROUP) + EPS)
    y_scn = (y_sc * _dot(_split_bf16(inv_g), expand) * scnw_ref[...]).astype(BF16)

    mix = _dot(y_ssd, wout_ref[0:width, :]) + _dot(y_scn, wout_ref[width:, :])
    o_ref[0] = x + g1 * mix


def _mixer(x, mod, n1w, win, cw, cb, dtb, alog, dskip, ssdnw, sccw, scnw, wout, btri, expand,
           gsum):
    bsz, seq, d = x.shape
    tl = MIX_TILE
    xbc_w = cw.shape[1]
    consts = (n1w, win, cw, cb, dtb, alog, dskip, ssdnw, sccw, scnw, wout, btri, expand, gsum)
    return pl.pallas_call(
        _mixer_kernel,
        out_shape=jax.ShapeDtypeStruct((bsz, seq, d), F32),
        grid=(bsz, seq // tl),
        in_specs=[
            pl.BlockSpec((1, tl, d), lambda b, l: (b, l, 0)),
            pl.BlockSpec((1, mod.shape[1], d), lambda b, l: (b, 0, 0)),
        ] + [_resident(a.shape) for a in consts],
        out_specs=pl.BlockSpec((1, tl, d), lambda b, l: (b, l, 0)),
        scratch_shapes=[
            pltpu.VMEM((tl + HALO, xbc_w), F32),
            pltpu.VMEM((tl + HALO, d), F32),
            pltpu.VMEM((SSD_STATE, d), F32),
            pltpu.VMEM((tl, d), F32),
            pltpu.VMEM((tl, d), F32),
            pltpu.VMEM((tl, d), F32),
            pltpu.VMEM((tl, xbc_w - d), F32),
            pltpu.VMEM((tl, d), F32),
            pltpu.VMEM((tl, d), F32),
        ],
        compiler_params=pltpu.CompilerParams(
            dimension_semantics=("arbitrary", "arbitrary"),
            vmem_limit_bytes=VMEM_LIMIT_BYTES),
        name="mixer",
    )(x, mod, *consts)


def _ffn_kernel(x_ref, mod_ref, n2w_ref, wup_ref, fcw_ref, fcb_ref, wdown_ref, fnw_ref,
                o_ref, ubuf, carry, act_s):
    tl = x_ref.shape[1]
    hidden = wdown_ref.shape[0]
    n_col_chunks = hidden // FFN_COLS

    @pl.when(pl.program_id(1) == 0)
    def _():
        carry[...] = jnp.zeros(carry.shape, F32)

    x = x_ref[0]
    sh2 = mod_ref[0, 3:4, :]
    sc2 = mod_ref[0, 4:5, :]
    g2 = mod_ref[0, 5:6, :]
    ms = jnp.mean(x * x, axis=-1, keepdims=True)
    h = (x * lax.rsqrt(ms + EPS)) * (n2w_ref[...] * (1.0 + sc2)) + sh2
    hb = h.astype(BF16)

    def up_conv(slot, col0):
        cs = slice(col0, col0 + FFN_COLS)
        ub = ubuf.at[slot]
        ub[0:HALO, :] = carry[:, cs]
        ub[HALO:HALO + tl, :] = _dot(hb, wup_ref[:, cs])
        w = fcw_ref[:, cs]
        u = (fcb_ref[:, cs]
             + w[2:3, :] * ub[HALO:HALO + tl, :]
             + w[1:2, :] * ub[HALO - 1:HALO - 1 + tl, :]
             + w[0:1, :] * ub[HALO - 2:HALO - 2 + tl, :])
        carry[:, cs] = ub[tl:tl + HALO, :]
        return u

    down = None
    group_start = 0
    for j in range(n_col_chunks):
        gate = up_conv((2 * j) % 4, j * FFN_COLS)
        value = up_conv((2 * j + 1) % 4, hidden + j * FFN_COLS)
        act_s[:, j * FFN_COLS:(j + 1) * FFN_COLS] = (_silu(gate) * value).astype(BF16)
        if (j + 1) % FFN_DOWN_GROUP == 0 or j == n_col_chunks - 1:
            ks = slice(group_start * FFN_COLS, (j + 1) * FFN_COLS)
            part = _dot(act_s[:, ks], wdown_ref[ks, :])
            down = part if down is None else down + part
            group_start = j + 1

    x2 = x + g2 * down
    ms2 = jnp.mean(x2 * x2, axis=-1, keepdims=True)
    o_ref[0] = (x2 * lax.rsqrt(ms2 + EPS)) * fnw_ref[...]


def _ffn(x, mod, n2w, wup, fcw, fcb, wdown, fnw):
    bsz, seq, d = x.shape
    tl = FFN_TILE
    hidden = wdown.shape[0]
    consts = (n2w, wup, fcw, fcb, wdown, fnw)
    return pl.pallas_call(
        _ffn_kernel,
        out_shape=jax.ShapeDtypeStruct((bsz, seq, d), F32),
        grid=(bsz, seq // tl),
        in_specs=[
            pl.BlockSpec((1, tl, d), lambda b, l: (b, l, 0)),
            pl.BlockSpec((1, mod.shape[1], d), lambda b, l: (b, 0, 0)),
        ] + [_resident(a.shape) for a in consts],
        out_specs=pl.BlockSpec((1, tl, d), lambda b, l: (b, l, 0)),
        scratch_shapes=[
            pltpu.VMEM((4, tl + HALO, FFN_COLS), F32),
            pltpu.VMEM((HALO, 2 * hidden), F32),
            pltpu.VMEM((tl, hidden), BF16),
        ],
        compiler_params=pltpu.CompilerParams(
            dimension_semantics=("arbitrary", "arbitrary"),
            vmem_limit_bytes=VMEM_LIMIT_BYTES),
        name="ffn",
    )(x, mod, *consts)


def kernel(x, c, w_ada, b_ada, norm1_w, w_in, ssd_conv_w, ssd_conv_b, dt_bias, a_log, d_skip,
           ssd_norm_w, sc_conv_w, sc_norm_w, w_out, norm2_w, w_up, ffn_conv_w, ffn_conv_b, w_down,
           final_norm_w):
    bsz, seq, d = x.shape
    heads = dt_bias.shape[1]
    xbc_w = ssd_conv_w.shape[2]
    hidden = w_down.shape[1]
    assert w_in.shape[0] == 1, "one layer: the final norm is fused into the ffn call"
    assert d % LANES == 0 and seq % FFN_TILE == 0 and seq % MIX_TILE == 0
    assert heads * HEAD_DIM == d and heads <= 2 * SUBLANES and hidden % FFN_COLS == 0
    assert xbc_w == d + 2 * SSD_GROUPS * SSD_STATE

    t = jnp.arange(MIX_TILE)
    btri = ((t[:, None] >= t[None, :]) & (t[:, None] // CHUNK == t[None, :] // CHUNK)).astype(BF16)
    ch = jnp.arange(d)
    j2 = jnp.arange(2 * LANES)
    expand = ((j2[:, None] % LANES) == (ch[None, :] // HEAD_DIM)).astype(BF16)
    gsum = ((ch[:, None] // SC_GROUP) == jnp.arange(LANES)[None, :]).astype(BF16)

    def pad_lanes(v):
        return jnp.pad(v, ((0, 0), (0, LANES - v.shape[1])))

    mod = _adaln_mod(c, w_ada[0], b_ada[0]).reshape(bsz, 6, d)
    wi = w_in[0]
    dt0 = d + xbc_w
    win = jnp.concatenate(
        [wi[:, :dt0], wi[:, dt0 + heads:], pad_lanes(wi[:, dt0:dt0 + heads])], axis=1).astype(BF16)
    x1 = _mixer(
        x, mod, norm1_w, win, ssd_conv_w[0], ssd_conv_b, pad_lanes(dt_bias), pad_lanes(a_log),
        jnp.repeat(d_skip[0], HEAD_DIM)[None], ssd_norm_w, sc_conv_w[0], sc_norm_w,
        w_out[0].astype(BF16), btri, expand, gsum)
    return _ffn(x1, mod, norm2_w, w_up[0].astype(BF16), ffn_conv_w[0], ffn_conv_b,
                w_down[0].astype(BF16), final_norm_w[None])
```

```python
import jax
import jax.numpy as jnp
from jax import lax
from jax.experimental import pallas as pl
from jax.experimental.pallas import tpu as pltpu

EPS = 1e-6
CHUNK = 64
HEAD_DIM = 64
SSD_GROUPS = 2
SSD_STATE = 128
SC_GROUP = 64
LANES = 128
SUBLANES = 8
HALO = SUBLANES

MIX_TILE = 256
SIDE_COLS = 256
SIDE_EVERY = 2
FFN_TILE = 512
FFN_COLS = 256

VMEM_LIMIT_BYTES = 56 * 1024 * 1024

BF16 = jnp.bfloat16
F32 = jnp.float32
NEG_BIG = -1e30


def _dot(a, b):
    return jnp.dot(a, b, preferred_element_type=F32)


def _sigmoid(v):
    return 1.0 / (1.0 + jnp.exp(-v))


def _silu(v):
    return v * _sigmoid(v)


def _split_bf16(v):
    hi = v.astype(BF16)
    lo = (v - hi.astype(F32)).astype(BF16)
    return jnp.concatenate([hi, lo], axis=1)


def _resident(shape):
    nd = len(shape)
    return pl.BlockSpec(shape, lambda *_: (0,) * nd, pipeline_mode=pl.Buffered(1))


def _lane_slabs(v):
    return [v[:, s * LANES:(s + 1) * LANES] for s in range(v.shape[1] // LANES)]


def _causal_conv_slab(buf, new_rows, taps, bias):
    tl = new_rows.shape[0]
    n_taps = len(taps)
    buf[HALO:HALO + tl, :] = new_rows
    out = []
    for parity in range(2):
        acc = bias
        for k in range(n_taps):
            rows = pl.ds(HALO - (n_taps - 1) + k + parity, tl // 2, stride=2)
            term = taps[k] * buf[rows, :]
            acc = term if acc is None else acc + term
        out.append(acc)
    buf[0:HALO, :] = buf[tl:tl + HALO, :]
    return out


def _mod_kernel(c_ref, w_ref, b_ref, o_ref):
    c = c_ref[...]
    o_ref[...] = _dot(_silu(c).astype(BF16), w_ref[...].astype(BF16)) + b_ref[...]


def _adaln_mod(c, w_ada, b_ada):
    bsz, d = c.shape
    n = w_ada.shape[1]
    return pl.pallas_call(
        _mod_kernel,
        out_shape=jax.ShapeDtypeStruct((bsz, n), F32),
        grid=(n // d,),
        in_specs=[
            pl.BlockSpec((bsz, d), lambda j: (0, 0)),
            pl.BlockSpec((d, d), lambda j: (0, j)),
            pl.BlockSpec((1, d), lambda j: (0, j)),
        ],
        out_specs=pl.BlockSpec((bsz, d), lambda j: (0, j)),
        compiler_params=pltpu.CompilerParams(dimension_semantics=("arbitrary",)),
        name="adaln_mod",
    )(c, w_ada, b_ada.reshape(1, n))


def _mixer_kernel(x_ref, mod_ref, n1w_ref, win_ref, cw_ref, cb_ref, dtb_ref, alog_ref,
                  dskip_ref, ssdnw_ref, sccw_ref, scnw_ref, wout_ref, btri_ref, expand_ref,
                  gsum_ref,
                  o_ref,
                  xbuf, pbuf, state, aexp_s, xdt_s, xbc_s, conv_s, y_s, gate_s, mixsc_s):
    tl = x_ref.shape[1]
    half = tl // 2
    width = x_ref.shape[2]
    n_chunks = tl // CHUNK
    gcols = width // SSD_GROUPS
    pairs_per_group = gcols // LANES
    x_slabs = width // LANES
    bc_slabs = SSD_GROUPS * SSD_STATE // LANES

    @pl.when(pl.program_id(1) == 0)
    def _():
        xbuf[:, 0:HALO, :] = jnp.zeros((xbuf.shape[0], HALO, LANES), F32)
        pbuf[:, 0:HALO, :] = jnp.zeros((pbuf.shape[0], HALO, LANES), F32)
        state[...] = jnp.zeros(state.shape, F32)

    x = x_ref[0]
    sh1 = mod_ref[0, 0:1, :]
    sc1 = mod_ref[0, 1:2, :]
    g1 = mod_ref[0, 2:3, :]
    ms = jnp.mean(x * x, axis=-1, keepdims=True)
    h = (x * lax.rsqrt(ms + EPS)) * (n1w_ref[...] * (1.0 + sc1)) + sh1
    hb = h.astype(BF16)

    xbc_w = xbuf.shape[0] * LANES
    col_z, col_xbc = 0, width
    col_scb = col_xbc + xbc_w
    col_scc, col_sch, col_dt = col_scb + width, col_scb + 2 * width, col_scb + 3 * width

    def in_proj(col0, ncols):
        return _dot(hb, win_ref[:, col0:col0 + ncols])

    side = {}
    n_fill = width // SIDE_COLS
    slabs_per_fill = SIDE_COLS // LANES

    def fill_sc_conv(k):
        c0 = k * SIDE_COLS
        prod = in_proj(col_scc + c0, SIDE_COLS) * in_proj(col_sch + c0, SIDE_COLS)
        sw = sccw_ref[:, c0:c0 + SIDE_COLS]
        for j, rows in enumerate(_lane_slabs(prod)):
            s = k * slabs_per_fill + j
            taps = [sw[t:t + 1, j * LANES:(j + 1) * LANES] for t in range(sw.shape[0])]
            for parity, acc in enumerate(_causal_conv_slab(pbuf.at[s], rows, taps, None)):
                conv_s[s, pl.ds(parity, half, stride=2), :] = acc

    def fill_sc_gate(k):
        scb = in_proj(col_scb + k * SIDE_COLS, SIDE_COLS)
        for j, rows in enumerate(_lane_slabs(scb)):
            s = k * slabs_per_fill + j
            conv_s[s] = rows * conv_s[s]

    def fill_sc_norm(_):
        y_sc = jnp.concatenate([conv_s[s] for s in range(conv_s.shape[0])], axis=1)
        ssq = _dot((y_sc * y_sc).astype(BF16), gsum_ref[...])
        inv_g = lax.rsqrt(ssq * (1.0 / SC_GROUP) + EPS)
        side["y_scn"] = (y_sc * _dot(_split_bf16(inv_g), expand_ref[:, 0:width])
                         * scnw_ref[...]).astype(BF16)

    def fill_gate(k):
        c0 = k * SIDE_COLS
        gate_s[:, c0:c0 + SIDE_COLS] = _silu(in_proj(col_z + c0, SIDE_COLS))

    def fill_sc_out(k):
        c0 = k * SIDE_COLS
        mixsc_s[:, c0:c0 + SIDE_COLS] = _dot(side["y_scn"], wout_ref[width:, c0:c0 + SIDE_COLS])

    fillers = ([(fill_sc_conv, k) for k in range(n_fill)]
               + [(fill_sc_gate, k) for k in range(n_fill)]
               + [(fill_sc_norm, 0)]
               + [(fill_gate, k) for k in range(n_fill)]
               + [(fill_sc_out, k) for k in range(n_fill)])

    cw = cw_ref[...]
    cb = cb_ref[...]
    for s, rows in enumerate(_lane_slabs(in_proj(col_xbc, xbc_w))):
        ls = slice(s * LANES, (s + 1) * LANES)
        taps = [cw[k:k + 1, ls] for k in range(cw.shape[0])]
        for parity, acc in enumerate(_causal_conv_slab(xbuf.at[s], rows, taps, cb[:, ls])):
            xbc_s[s, pl.ds(parity, half, stride=2), :] = _silu(acc)
    xs = jnp.concatenate([xbc_s[s] for s in range(x_slabs)], axis=1)

    dtv = in_proj(col_dt, LANES) + dtb_ref[...]
    dt = jnp.maximum(dtv, 0.0) + jnp.log1p(jnp.exp(-jnp.abs(dtv)))
    a = dt * (-jnp.exp(alog_ref[...]))
    acs2 = _dot(btri_ref[...], _split_bf16(a))
    a_cs = acs2[:, 0:LANES] + acs2[:, LANES:]

    expand = expand_ref[:, 0:width]
    aexp_s[...] = _dot(_split_bf16(a_cs), expand)
    xdt_s[...] = xs * _dot(_split_bf16(dt), expand)

    acs_t = a_cs.T
    acs_t_rot = pltpu.roll(acs_t, CHUNK, axis=1)
    lane16 = lax.broadcasted_iota(jnp.int32, (2 * SUBLANES, LANES), 1)
    blocks = tl // LANES
    chunk_rows = []
    for c in range(n_chunks):
        if c % 2 == 0:
            left = acs_t[0:16, (c // 2) * LANES:(c // 2 + 1) * LANES]
            right = acs_t_rot[1:17, (c // 2) * LANES:(c // 2 + 1) * LANES]
        else:
            nb = ((c + 1) // 2) % blocks
            left = acs_t_rot[0:16, nb * LANES:(nb + 1) * LANES]
            right = acs_t[1:17, (c // 2) * LANES:(c // 2 + 1) * LANES]
        chunk_rows.append(jnp.where(lane16 < CHUNK, left, right))

    lane = lax.broadcasted_iota(jnp.int32, (CHUNK, LANES), 1)
    srow = lax.broadcasted_iota(jnp.int32, (CHUNK, LANES), 0)
    tri2 = srow >= (lane & (CHUNK - 1))
    left_half = lane < CHUNK

    for c in range(n_chunks):
        rows = slice(c * CHUNK, (c + 1) * CHUNK)
        aexp_c = aexp_s[rows, :]
        last = aexp_s[(c + 1) * CHUNK - 1:(c + 1) * CHUNK, :]
        dec_start = jnp.exp(aexp_c)
        dec_end = jnp.exp(last - aexp_c)
        chunk_decay = jnp.exp(last)
        xdt_c = xdt_s[rows, :]
        xw_b = (xdt_c * dec_end).astype(BF16)
        bm_b = [xbc_s[x_slabs + g, rows, :].astype(BF16) for g in range(SSD_GROUPS)]
        cm_b = [xbc_s[x_slabs + SSD_GROUPS + g, rows, :].astype(BF16) for g in range(SSD_GROUPS)]
        prev = state[...]
        prev_b = prev.astype(BF16)
        for g in range(SSD_GROUPS):
            gs = slice(g * gcols, (g + 1) * gcols)
            cmg, bmg = cm_b[g], bm_b[g]
            cb2 = lax.dot_general(cmg, jnp.concatenate([bmg, bmg], axis=0),
                                  (((1,), (1,)), ((), ())), preferred_element_type=F32)
            y_off = _dot(cmg, prev_b[:, gs])
            s_new = lax.dot_general(bmg, xw_b[:, gs], (((0,), (0,)), ((), ())),
                                    preferred_element_type=F32)
            state[:, gs] = prev[:, gs] * chunk_decay[:, gs] + s_new
            for hp in range(pairs_per_group):
                p = g * pairs_per_group + hp
                cs = slice(p * LANES, (p + 1) * LANES)
                seg = aexp_c[:, cs] - chunk_rows[c][2 * p:2 * p + 1, :]
                m2 = (jnp.exp(jnp.where(tri2, seg, NEG_BIG)) * cb2).astype(BF16)
                xd2 = xdt_c[:, cs]
                rhs = jnp.concatenate([jnp.where(left_half, xd2, 0.0),
                                       jnp.where(left_half, 0.0, xd2)], axis=0).astype(BF16)
                y_diag = _dot(m2, rhs)
                y_s[rows, cs] = y_diag + y_off[:, hp * LANES:(hp + 1) * LANES] * dec_start[:, cs]
                if p % SIDE_EVERY == SIDE_EVERY - 1 and fillers:
                    fn, k = fillers.pop(0)
                    fn(k)
    for fn, k in fillers:
        fn(k)

    y = (y_s[...] + dskip_ref[...] * xs) * gate_s[...]
    parts = []
    for g in range(SSD_GROUPS):
        yg = y[:, g * gcols:(g + 1) * gcols]
        inv = lax.rsqrt(jnp.mean(yg * yg, axis=-1, keepdims=True) + EPS)
        parts.append(yg * inv)
    y_ssd = (jnp.concatenate(parts, axis=1) * ssdnw_ref[...]).astype(BF16)

    mix = _dot(y_ssd, wout_ref[0:width, 0:width]) + mixsc_s[...]
    o_ref[0] = x + g1 * mix


def _mixer(x, mod, n1w, win, cw, cb, dtb, alog, dskip, ssdnw, sccw, scnw, wout, btri, expand,
           gsum):
    bsz, seq, d = x.shape
    tl = MIX_TILE
    xbc_w = cw.shape[1]
    consts = (n1w, win, cw, cb, dtb, alog, dskip, ssdnw, sccw, scnw, wout, btri, expand, gsum)
    return pl.pallas_call(
        _mixer_kernel,
        out_shape=jax.ShapeDtypeStruct((bsz, seq, d), F32),
        grid=(bsz, seq // tl),
        in_specs=[
            pl.BlockSpec((1, tl, d), lambda b, l: (b, l, 0)),
            pl.BlockSpec((1, mod.shape[1], d), lambda b, l: (b, 0, 0)),
        ] + [_resident(a.shape) for a in consts],
        out_specs=pl.BlockSpec((1, tl, d), lambda b, l: (b, l, 0)),
        scratch_shapes=[
            pltpu.VMEM((xbc_w // LANES, tl + HALO, LANES), F32),
            pltpu.VMEM((d // LANES, tl + HALO, LANES), F32),
            pltpu.VMEM((SSD_STATE, d), F32),
            pltpu.VMEM((tl, d), F32),
            pltpu.VMEM((tl, d), F32),
            pltpu.VMEM((xbc_w // LANES, tl, LANES), F32),
            pltpu.VMEM((d // LANES, tl, LANES), F32),
            pltpu.VMEM((tl, d), F32),
            pltpu.VMEM((tl, d), F32),
            pltpu.VMEM((tl, d), F32),
        ],
        compiler_params=pltpu.CompilerParams(
            dimension_semantics=("arbitrary", "arbitrary"),
            vmem_limit_bytes=VMEM_LIMIT_BYTES),
        name="mixer",
    )(x, mod, *consts)


def _ffn_kernel(x_ref, mod_ref, n2w_ref, wup_ref, fcw_ref, fcb_ref, wdown_ref, fnw_ref,
                o_ref, ubuf, act_s, dn_s):
    tl = x_ref.shape[1]
    half = tl // 2
    d = x_ref.shape[2]
    hidden = wdown_ref.shape[0]
    n_col_chunks = hidden // FFN_COLS
    slabs_per_chunk = FFN_COLS // LANES

    @pl.when(pl.program_id(1) == 0)
    def _():
        ubuf[:, 0:HALO, :] = jnp.zeros((ubuf.shape[0], HALO, LANES), F32)

    x = x_ref[0]
    sh2 = mod_ref[0, 3:4, :]
    sc2 = mod_ref[0, 4:5, :]
    g2 = mod_ref[0, 5:6, :]
    ms = jnp.mean(x * x, axis=-1, keepdims=True)
    h = (x * lax.rsqrt(ms + EPS)) * (n2w_ref[...] * (1.0 + sc2)) + sh2
    hb = h.astype(BF16)

    def up_conv(col0):
        up = _dot(hb, wup_ref[:, col0:col0 + FFN_COLS])
        cols = []
        for s, rows in enumerate(_lane_slabs(up)):
            slab = col0 // LANES + s
            ls = slice(slab * LANES, (slab + 1) * LANES)
            w = fcw_ref[:, ls]
            taps = [w[k:k + 1, :] for k in range(w.shape[0])]
            even, odd = _causal_conv_slab(ubuf.at[slab], rows, taps, fcb_ref[:, ls])
            cols.append(jnp.concatenate([even, odd], axis=0))
        return jnp.concatenate(cols, axis=1)

    for j in range(n_col_chunks):
        gate = up_conv(j * FFN_COLS)
        value = up_conv(hidden + j * FFN_COLS)
        act_s[:, j * FFN_COLS:(j + 1) * FFN_COLS] = (_silu(gate) * value).astype(BF16)

    down = _dot(act_s[...], wdown_ref[:, 0:d])
    for s, rows in enumerate(_lane_slabs(down)):
        dn_s[s, pl.ds(0, half, stride=2), :] = rows[0:half, :]
        dn_s[s, pl.ds(1, half, stride=2), :] = rows[half:, :]
    x2 = x + g2 * jnp.concatenate([dn_s[s] for s in range(d // LANES)], axis=1)
    ms2 = jnp.mean(x2 * x2, axis=-1, keepdims=True)
    o_ref[0] = (x2 * lax.rsqrt(ms2 + EPS)) * fnw_ref[...]


def _ffn(x, mod, n2w, wup, fcw, fcb, wdown, fnw):
    bsz, seq, d = x.shape
    tl = FFN_TILE
    hidden = wdown.shape[0]
    consts = (n2w, wup, fcw, fcb, wdown, fnw)
    return pl.pallas_call(
        _ffn_kernel,
        out_shape=jax.ShapeDtypeStruct((bsz, seq, d), F32),
        grid=(bsz, seq // tl),
        in_specs=[
            pl.BlockSpec((1, tl, d), lambda b, l: (b, l, 0)),
            pl.BlockSpec((1, mod.shape[1], d), lambda b, l: (b, 0, 0)),
        ] + [_resident(a.shape) for a in consts],
        out_specs=pl.BlockSpec((1, tl, d), lambda b, l: (b, l, 0)),
        scratch_shapes=[
            pltpu.VMEM((2 * hidden // LANES, tl + HALO, LANES), F32),
            pltpu.VMEM((tl, hidden), BF16),
            pltpu.VMEM((d // LANES, tl, LANES), F32),
        ],
        compiler_params=pltpu.CompilerParams(
            dimension_semantics=("arbitrary", "arbitrary"),
            vmem_limit_bytes=VMEM_LIMIT_BYTES),
        name="ffn",
    )(x, mod, *consts)


def kernel(x, c, w_ada, b_ada, norm1_w, w_in, ssd_conv_w, ssd_conv_b, dt_bias, a_log, d_skip,
           ssd_norm_w, sc_conv_w, sc_norm_w, w_out, norm2_w, w_up, ffn_conv_w, ffn_conv_b, w_down,
           final_norm_w):
    bsz, seq, d = x.shape
    heads = dt_bias.shape[1]
    xbc_w = ssd_conv_w.shape[2]
    hidden = w_down.shape[1]
    assert w_in.shape[0] == 1, "one layer: the final norm is fused into the ffn call"
    assert d % LANES == 0 and seq % FFN_TILE == 0 and seq % MIX_TILE == 0
    assert heads * HEAD_DIM == d and heads <= 2 * SUBLANES and hidden % FFN_COLS == 0
    assert xbc_w == d + 2 * SSD_GROUPS * SSD_STATE

    t = jnp.arange(MIX_TILE)
    btri = ((t[:, None] >= t[None, :]) & (t[:, None] // CHUNK == t[None, :] // CHUNK)).astype(BF16)
    ch = jnp.arange(d)
    j2 = jnp.arange(2 * LANES)
    expand = ((j2[:, None] % LANES) == (ch[None, :] // HEAD_DIM)).astype(BF16)
    gsum = ((ch[:, None] // SC_GROUP) == jnp.arange(LANES)[None, :]).astype(BF16)

    def pad_lanes(v):
        return jnp.pad(v, ((0, 0), (0, LANES - v.shape[1])))

    def widen(w):
        return jnp.pad(w, ((0, 0), (0, LANES))) if w.shape[1] % (8 * LANES) == 0 else w

    mod = _adaln_mod(c, w_ada[0], b_ada[0]).reshape(bsz, 6, d)
    wi = w_in[0]
    dt0 = d + xbc_w
    win = jnp.concatenate(
        [wi[:, :dt0], wi[:, dt0 + heads:], pad_lanes(wi[:, dt0:dt0 + heads])], axis=1).astype(BF16)
    x1 = _mixer(
        x, mod, norm1_w, win, ssd_conv_w[0], ssd_conv_b, pad_lanes(dt_bias), pad_lanes(a_log),
        jnp.repeat(d_skip[0], HEAD_DIM)[None], ssd_norm_w, sc_conv_w[0], sc_norm_w,
        widen(w_out[0].astype(BF16)), btri, widen(expand), gsum)
    return _ffn(x1, mod, norm2_w, w_up[0].astype(BF16), ffn_conv_w[0], ffn_conv_b,
                widen(w_down[0].astype(BF16)), final_norm_w[None])
```

```python
import jax
import jax.numpy as jnp
from jax import lax
from jax.experimental import pallas as pl
from jax.experimental.pallas import tpu as pltpu

EPS = 1e-6
HEAD_DIM = 64
SSD_GROUPS = 2
SSD_STATE = 128
SC_GROUP = 64
LANES = 128
SUBLANES = 8
HALO = SUBLANES

MIX_TILE = 256
SIDE_COLS = 256
SIDE_BEFORE_CONV, SIDE_BEFORE_SCAN, SIDE_PER_PAIR = 4, 2, 2
FFN_TILE = 512
FFN_COLS = 256

VMEM_LIMIT_BYTES = 56 * 1024 * 1024

BF16 = jnp.bfloat16
F32 = jnp.float32
NEG_BIG = -1e30


def _dot(a, b):
    return jnp.dot(a, b, preferred_element_type=F32)


def _sigmoid(v):
    return 1.0 / (1.0 + jnp.exp(-v))


def _silu(v):
    return v * _sigmoid(v)


def _split_bf16(v):
    hi = v.astype(BF16)
    lo = (v - hi.astype(F32)).astype(BF16)
    return jnp.concatenate([hi, lo], axis=1)


def _resident(shape):
    nd = len(shape)
    return pl.BlockSpec(shape, lambda *_: (0,) * nd, pipeline_mode=pl.Buffered(1))


def _lane_slabs(v):
    return [v[:, s * LANES:(s + 1) * LANES] for s in range(v.shape[1] // LANES)]


def _causal_conv_slab(buf, new_rows, taps, bias):
    tl = new_rows.shape[0]
    n_taps = len(taps)
    buf[HALO:HALO + tl, :] = new_rows
    out = []
    for parity in range(2):
        acc = bias
        for k in range(n_taps):
            rows = pl.ds(HALO - (n_taps - 1) + k + parity, tl // 2, stride=2)
            term = taps[k] * buf[rows, :]
            acc = term if acc is None else acc + term
        out.append(acc)
    buf[0:HALO, :] = buf[tl:tl + HALO, :]
    return out


def _mod_kernel(c_ref, w_ref, b_ref, o_ref):
    c = c_ref[...]
    o_ref[...] = _dot(_silu(c).astype(BF16), w_ref[...].astype(BF16)) + b_ref[...]


def _adaln_mod(c, w_ada, b_ada):
    bsz, d = c.shape
    n = w_ada.shape[1]
    return pl.pallas_call(
        _mod_kernel,
        out_shape=jax.ShapeDtypeStruct((bsz, n), F32),
        grid=(n // d,),
        in_specs=[
            pl.BlockSpec((bsz, d), lambda j: (0, 0)),
            pl.BlockSpec((d, d), lambda j: (0, j)),
            pl.BlockSpec((1, d), lambda j: (0, j)),
        ],
        out_specs=pl.BlockSpec((bsz, d), lambda j: (0, j)),
        compiler_params=pltpu.CompilerParams(dimension_semantics=("arbitrary",)),
        name="adaln_mod",
    )(c, w_ada, b_ada.reshape(1, n))


def _mixer_kernel(x_ref, mod_ref, n1w_ref, win_ref, cw_ref, cb_ref, dtb_ref, alog_ref,
                  dskip_ref, ssdnw_ref, sccw_ref, scnw_ref, wout_ref, btri_ref, expand_ref,
                  gsum_ref,
                  o_ref,
                  xbuf, pbuf, state, aexp_s, xdt_s, xbc_s, conv_s, y_s, gate_s, mixsc_s):
    tl = x_ref.shape[1]
    half = tl // 2
    width = x_ref.shape[2]
    gcols = width // SSD_GROUPS
    pairs_per_group = gcols // LANES
    x_slabs = width // LANES

    @pl.when(pl.program_id(1) == 0)
    def _():
        xbuf[:, 0:HALO, :] = jnp.zeros((xbuf.shape[0], HALO, LANES), F32)
        pbuf[:, 0:HALO, :] = jnp.zeros((pbuf.shape[0], HALO, LANES), F32)
        state[...] = jnp.zeros(state.shape, F32)

    x = x_ref[0]
    sh1 = mod_ref[0, 0:1, :]
    sc1 = mod_ref[0, 1:2, :]
    g1 = mod_ref[0, 2:3, :]
    ms = jnp.mean(x * x, axis=-1, keepdims=True)
    h = (x * lax.rsqrt(ms + EPS)) * (n1w_ref[...] * (1.0 + sc1)) + sh1
    hb = h.astype(BF16)

    xbc_w = xbuf.shape[0] * LANES
    col_z, col_xbc = 0, width
    col_scb = col_xbc + xbc_w
    col_scc, col_sch, col_dt = col_scb + width, col_scb + 2 * width, col_scb + 3 * width

    def in_proj(col0, ncols):
        return _dot(hb, win_ref[:, col0:col0 + ncols])

    side = {}
    n_fill = width // SIDE_COLS
    slabs_per_fill = SIDE_COLS // LANES

    def fill_sc_conv(k):
        c0 = k * SIDE_COLS
        prod = in_proj(col_scc + c0, SIDE_COLS) * in_proj(col_sch + c0, SIDE_COLS)
        sw = sccw_ref[:, c0:c0 + SIDE_COLS]
        for j, rows in enumerate(_lane_slabs(prod)):
            s = k * slabs_per_fill + j
            taps = [sw[t:t + 1, j * LANES:(j + 1) * LANES] for t in range(sw.shape[0])]
            for parity, acc in enumerate(_causal_conv_slab(pbuf.at[s], rows, taps, None)):
                conv_s[s, pl.ds(parity, half, stride=2), :] = acc

    def fill_sc_gate(k):
        scb = in_proj(col_scb + k * SIDE_COLS, SIDE_COLS)
        for j, rows in enumerate(_lane_slabs(scb)):
            s = k * slabs_per_fill + j
            conv_s[s] = rows * conv_s[s]

    def fill_sc_norm(_):
        y_sc = jnp.concatenate([conv_s[s] for s in range(conv_s.shape[0])], axis=1)
        ssq = _dot((y_sc * y_sc).astype(BF16), gsum_ref[...])
        inv_g = lax.rsqrt(ssq * (1.0 / SC_GROUP) + EPS)
        side["y_scn"] = (y_sc * _dot(_split_bf16(inv_g), expand_ref[:, 0:width])
                         * scnw_ref[...]).astype(BF16)

    def fill_gate(k):
        c0 = k * SIDE_COLS
        gate_s[:, c0:c0 + SIDE_COLS] = _silu(in_proj(col_z + c0, SIDE_COLS))

    def fill_sc_out(k):
        c0 = k * SIDE_COLS
        mixsc_s[:, c0:c0 + SIDE_COLS] = _dot(side["y_scn"], wout_ref[width:, c0:c0 + SIDE_COLS])

    side_work = ([(fill_sc_conv, k) for k in range(n_fill)]
                 + [(fill_sc_gate, k) for k in range(n_fill)]
                 + [(fill_sc_norm, 0)]
                 + [(fill_gate, k) for k in range(n_fill)]
                 + [(fill_sc_out, k) for k in range(n_fill)])

    def run_side(n):
        for _ in range(min(n, len(side_work))):
            fn, k = side_work.pop(0)
            fn(k)

    cw = cw_ref[...]
    cb = cb_ref[...]
    xbc_raw = in_proj(col_xbc, xbc_w)
    dtv = in_proj(col_dt, LANES) + dtb_ref[...]
    run_side(SIDE_BEFORE_CONV)
    for s, rows in enumerate(_lane_slabs(xbc_raw)):
        ls = slice(s * LANES, (s + 1) * LANES)
        taps = [cw[k:k + 1, ls] for k in range(cw.shape[0])]
        for parity, acc in enumerate(_causal_conv_slab(xbuf.at[s], rows, taps, cb[:, ls])):
            xbc_s[s, pl.ds(parity, half, stride=2), :] = _silu(acc)
    xs = jnp.concatenate([xbc_s[s] for s in range(x_slabs)], axis=1)

    run_side(SIDE_BEFORE_SCAN)
    dt = jnp.maximum(dtv, 0.0) + jnp.log1p(jnp.exp(-jnp.abs(dtv)))
    a = dt * (-jnp.exp(alog_ref[...]))
    acs2 = _dot(btri_ref[...], _split_bf16(a))
    a_cs = acs2[:, 0:LANES] + acs2[:, LANES:]

    expand = expand_ref[:, 0:width]
    aexp_s[...] = _dot(_split_bf16(a_cs), expand)
    xdt_s[...] = xs * _dot(_split_bf16(dt), expand)

    acs_t = a_cs.T
    aexp = aexp_s[...]
    last = aexp_s[tl - 1:tl, :]
    dec_start = jnp.exp(aexp)
    xdt = xdt_s[...]
    xw_b = (xdt * jnp.exp(last - aexp)).astype(BF16)
    tile_decay = jnp.exp(last)
    bm_b = [xbc_s[x_slabs + g].astype(BF16) for g in range(SSD_GROUPS)]
    cm_b = [xbc_s[x_slabs + SSD_GROUPS + g].astype(BF16) for g in range(SSD_GROUPS)]
    prev = state[...]
    prev_b = prev.astype(BF16)

    srow = lax.broadcasted_iota(jnp.int32, (tl, tl), 0)
    scol = lax.broadcasted_iota(jnp.int32, (tl, tl), 1)
    causal = srow >= scol
    left_half = lax.broadcasted_iota(jnp.int32, (tl, LANES), 1) < HEAD_DIM

    for g in range(SSD_GROUPS):
        gs = slice(g * gcols, (g + 1) * gcols)
        cb = lax.dot_general(cm_b[g], bm_b[g], (((1,), (1,)), ((), ())),
                             preferred_element_type=F32)
        y_off = _dot(cm_b[g], prev_b[:, gs])
        s_new = lax.dot_general(bm_b[g], xw_b[:, gs], (((0,), (0,)), ((), ())),
                                preferred_element_type=F32)
        state[:, gs] = prev[:, gs] * tile_decay[:, gs] + s_new
        for hp in range(pairs_per_group):
            p = g * pairs_per_group + hp
            cs = slice(p * LANES, (p + 1) * LANES)
            decay = []
            for h in (2 * p, 2 * p + 1):
                seg = a_cs[:, h:h + 1] - acs_t[h:h + 1, :]
                decay.append((jnp.exp(jnp.where(causal, seg, NEG_BIG)) * cb).astype(BF16))
            xd2 = xdt[:, cs]
            rhs = jnp.concatenate([jnp.where(left_half, xd2, 0.0),
                                   jnp.where(left_half, 0.0, xd2)], axis=0).astype(BF16)
            y_diag = _dot(jnp.concatenate(decay, axis=1), rhs)
            y_s[:, cs] = y_diag + y_off[:, hp * LANES:(hp + 1) * LANES] * dec_start[:, cs]
            run_side(SIDE_PER_PAIR)
    run_side(len(side_work))

    y = (y_s[...] + dskip_ref[...] * xs) * gate_s[...]
    parts = []
    for g in range(SSD_GROUPS):
        yg = y[:, g * gcols:(g + 1) * gcols]
        inv = lax.rsqrt(jnp.mean(yg * yg, axis=-1, keepdims=True) + EPS)
        parts.append(yg * inv)
    y_ssd = (jnp.concatenate(parts, axis=1) * ssdnw_ref[...]).astype(BF16)

    mix = _dot(y_ssd, wout_ref[0:width, 0:width]) + mixsc_s[...]
    o_ref[0] = x + g1 * mix


def _mixer(x, mod, n1w, win, cw, cb, dtb, alog, dskip, ssdnw, sccw, scnw, wout, btri, expand,
           gsum):
    bsz, seq, d = x.shape
    tl = MIX_TILE
    xbc_w = cw.shape[1]
    consts = (n1w, win, cw, cb, dtb, alog, dskip, ssdnw, sccw, scnw, wout, btri, expand, gsum)
    return pl.pallas_call(
        _mixer_kernel,
        out_shape=jax.ShapeDtypeStruct((bsz, seq, d), F32),
        grid=(bsz, seq // tl),
        in_specs=[
            pl.BlockSpec((1, tl, d), lambda b, l: (b, l, 0)),
            pl.BlockSpec((1, mod.shape[1], d), lambda b, l: (b, 0, 0)),
        ] + [_resident(a.shape) for a in consts],
        out_specs=pl.BlockSpec((1, tl, d), lambda b, l: (b, l, 0)),
        scratch_shapes=[
            pltpu.VMEM((xbc_w // LANES, tl + HALO, LANES), F32),
            pltpu.VMEM((d // LANES, tl + HALO, LANES), F32),
            pltpu.VMEM((SSD_STATE, d), F32),
            pltpu.VMEM((tl, d), F32),
            pltpu.VMEM((tl, d), F32),
            pltpu.VMEM((xbc_w // LANES, tl, LANES), F32),
            pltpu.VMEM((d // LANES, tl, LANES), F32),
            pltpu.VMEM((tl, d), F32),
            pltpu.VMEM((tl, d), F32),
            pltpu.VMEM((tl, d), F32),
        ],
        compiler_params=pltpu.CompilerParams(
            dimension_semantics=("arbitrary", "arbitrary"),
            vmem_limit_bytes=VMEM_LIMIT_BYTES),
        name="mixer",
    )(x, mod, *consts)


def _ffn_kernel(x_ref, mod_ref, n2w_ref, wup_ref, fcw_ref, fcb_ref, wdown_ref, fnw_ref,
                o_ref, ubuf, act_s, dn_s):
    tl = x_ref.shape[1]
    half = tl // 2
    d = x_ref.shape[2]
    hidden = wdown_ref.shape[0]
    n_col_chunks = hidden // FFN_COLS
    slabs_per_chunk = FFN_COLS // LANES

    @pl.when(pl.program_id(1) == 0)
    def _():
        ubuf[:, 0:HALO, :] = jnp.zeros((ubuf.shape[0], HALO, LANES), F32)

    x = x_ref[0]
    sh2 = mod_ref[0, 3:4, :]
    sc2 = mod_ref[0, 4:5, :]
    g2 = mod_ref[0, 5:6, :]
    ms = jnp.mean(x * x, axis=-1, keepdims=True)
    h = (x * lax.rsqrt(ms + EPS)) * (n2w_ref[...] * (1.0 + sc2)) + sh2
    hb = h.astype(BF16)

    def up_conv(col0):
        up = _dot(hb, wup_ref[:, col0:col0 + FFN_COLS])
        cols = []
        for s, rows in enumerate(_lane_slabs(up)):
            slab = col0 // LANES + s
            ls = slice(slab * LANES, (slab + 1) * LANES)
            w = fcw_ref[:, ls]
            taps = [w[k:k + 1, :] for k in range(w.shape[0])]
            even, odd = _causal_conv_slab(ubuf.at[slab], rows, taps, fcb_ref[:, ls])
            cols.append(jnp.concatenate([even, odd], axis=0))
        return jnp.concatenate(cols, axis=1)

    for j in range(n_col_chunks):
        gate = up_conv(j * FFN_COLS)
        value = up_conv(hidden + j * FFN_COLS)
        act_s[:, j * FFN_COLS:(j + 1) * FFN_COLS] = (_silu(gate) * value).astype(BF16)

    down = _dot(act_s[...], wdown_ref[:, 0:d])
    for s, rows in enumerate(_lane_slabs(down)):
        dn_s[s, pl.ds(0, half, stride=2), :] = rows[0:half, :]
        dn_s[s, pl.ds(1, half, stride=2), :] = rows[half:, :]
    x2 = x + g2 * jnp.concatenate([dn_s[s] for s in range(d // LANES)], axis=1)
    ms2 = jnp.mean(x2 * x2, axis=-1, keepdims=True)
    o_ref[0] = (x2 * lax.rsqrt(ms2 + EPS)) * fnw_ref[...]


def _ffn(x, mod, n2w, wup, fcw, fcb, wdown, fnw):
    bsz, seq, d = x.shape
    tl = FFN_TILE
    hidden = wdown.shape[0]
    consts = (n2w, wup, fcw, fcb, wdown, fnw)
    return pl.pallas_call(
        _ffn_kernel,
        out_shape=jax.ShapeDtypeStruct((bsz, seq, d), F32),
        grid=(bsz, seq // tl),
        in_specs=[
            pl.BlockSpec((1, tl, d), lambda b, l: (b, l, 0)),
            pl.BlockSpec((1, mod.shape[1], d), lambda b, l: (b, 0, 0)),
        ] + [_resident(a.shape) for a in consts],
        out_specs=pl.BlockSpec((1, tl, d), lambda b, l: (b, l, 0)),
        scratch_shapes=[
            pltpu.VMEM((2 * hidden // LANES, tl + HALO, LANES), F32),
            pltpu.VMEM((tl, hidden), BF16),
            pltpu.VMEM((d // LANES, tl, LANES), F32),
        ],
        compiler_params=pltpu.CompilerParams(
            dimension_semantics=("arbitrary", "arbitrary"),
            vmem_limit_bytes=VMEM_LIMIT_BYTES),
        name="ffn",
    )(x, mod, *consts)


def kernel(x, c, w_ada, b_ada, norm1_w, w_in, ssd_conv_w, ssd_conv_b, dt_bias, a_log, d_skip,
           ssd_norm_w, sc_conv_w, sc_norm_w, w_out, norm2_w, w_up, ffn_conv_w, ffn_conv_b, w_down,
           final_norm_w):
    bsz, seq, d = x.shape
    heads = dt_bias.shape[1]
    xbc_w = ssd_conv_w.shape[2]
    hidden = w_down.shape[1]
    assert w_in.shape[0] == 1, "one layer: the final norm is fused into the ffn call"
    assert d % LANES == 0 and seq % FFN_TILE == 0 and seq % MIX_TILE == 0
    assert heads * HEAD_DIM == d and heads <= 2 * SUBLANES and hidden % FFN_COLS == 0
    assert xbc_w == d + 2 * SSD_GROUPS * SSD_STATE

    t = jnp.arange(MIX_TILE)
    btri = (t[:, None] >= t[None, :]).astype(BF16)
    ch = jnp.arange(d)
    j2 = jnp.arange(2 * LANES)
    expand = ((j2[:, None] % LANES) == (ch[None, :] // HEAD_DIM)).astype(BF16)
    gsum = ((ch[:, None] // SC_GROUP) == jnp.arange(LANES)[None, :]).astype(BF16)

    def pad_lanes(v):
        return jnp.pad(v, ((0, 0), (0, LANES - v.shape[1])))

    def widen(w):
        return jnp.pad(w, ((0, 0), (0, LANES))) if w.shape[1] % (8 * LANES) == 0 else w

    mod = _adaln_mod(c, w_ada[0], b_ada[0]).reshape(bsz, 6, d)
    wi = w_in[0]
    dt0 = d + xbc_w
    win = jnp.concatenate(
        [wi[:, :dt0], wi[:, dt0 + heads:], pad_lanes(wi[:, dt0:dt0 + heads])], axis=1).astype(BF16)
    x1 = _mixer(
        x, mod, norm1_w, win, ssd_conv_w[0], ssd_conv_b, pad_lanes(dt_bias), pad_lanes(a_log),
        jnp.repeat(d_skip[0], HEAD_DIM)[None], ssd_norm_w, sc_conv_w[0], sc_norm_w,
        widen(w_out[0].astype(BF16)), btri, widen(expand), gsum)
    return _ffn(x1, mod, norm2_w, w_up[0].astype(BF16), ffn_conv_w[0], ffn_conv_b,
                widen(w_down[0].astype(BF16)), final_norm_w[None])
```

```python
import jax
import jax.numpy as jnp
from jax import lax
from jax.experimental import pallas as pl
from jax.experimental.pallas import tpu as pltpu

EPS = 1e-6
HEAD_DIM = 64
SSD_GROUPS = 2
SSD_STATE = 128
SC_GROUP = 64
LANES = 128
SUBLANES = 8
HALO = SUBLANES

MIX_TILE = 512
SCAN_BLOCK = 256
SIDE_COLS = 256
SIDE_BEFORE_CONV, SIDE_BEFORE_SCAN, SIDE_PER_PAIR = 8, 2, 1
FFN_TILE = 512
FFN_COLS = 256

VMEM_LIMIT_BYTES = 56 * 1024 * 1024

BF16 = jnp.bfloat16
F32 = jnp.float32
NEG_BIG = -1e30


def _dot(a, b):
    return jnp.dot(a, b, preferred_element_type=F32)


def _sigmoid(v):
    return 1.0 / (1.0 + jnp.exp(-v))


def _silu(v):
    return v * _sigmoid(v)


def _split_bf16(v):
    hi = v.astype(BF16)
    lo = (v - hi.astype(F32)).astype(BF16)
    return jnp.concatenate([hi, lo], axis=1)


def _resident(shape):
    nd = len(shape)
    return pl.BlockSpec(shape, lambda *_: (0,) * nd, pipeline_mode=pl.Buffered(1))


def _lane_slabs(v):
    return [v[:, s * LANES:(s + 1) * LANES] for s in range(v.shape[1] // LANES)]


def _causal_conv_slab(buf, new_rows, taps, bias):
    tl = new_rows.shape[0]
    n_taps = len(taps)
    buf[HALO:HALO + tl, :] = new_rows
    out = []
    for parity in range(2):
        acc = bias
        for k in range(n_taps):
            rows = pl.ds(HALO - (n_taps - 1) + k + parity, tl // 2, stride=2)
            term = taps[k] * buf[rows, :]
            acc = term if acc is None else acc + term
        out.append(acc)
    buf[0:HALO, :] = buf[tl:tl + HALO, :]
    return out


def _mod_kernel(c_ref, w_ref, b_ref, o_ref):
    c = c_ref[...]
    o_ref[...] = _dot(_silu(c).astype(BF16), w_ref[...].astype(BF16)) + b_ref[...]


def _adaln_mod(c, w_ada, b_ada):
    bsz, d = c.shape
    n = w_ada.shape[1]
    return pl.pallas_call(
        _mod_kernel,
        out_shape=jax.ShapeDtypeStruct((bsz, n), F32),
        grid=(n // d,),
        in_specs=[
            pl.BlockSpec((bsz, d), lambda j: (0, 0)),
            pl.BlockSpec((d, d), lambda j: (0, j)),
            pl.BlockSpec((1, d), lambda j: (0, j)),
        ],
        out_specs=pl.BlockSpec((bsz, d), lambda j: (0, j)),
        compiler_params=pltpu.CompilerParams(dimension_semantics=("arbitrary",)),
        name="adaln_mod",
    )(c, w_ada, b_ada.reshape(1, n))


def _mixer_kernel(x_ref, mod_ref, n1w_ref, wa_ref, wb_ref, wdt_ref, cw_ref, cb_ref, dtb_ref,
                  alog_ref, dskip_ref, ssdnw_ref, sccw_ref, scnw_ref, wout_ref, btri_ref,
                  expand_ref, gsum_ref,
                  o_ref,
                  xbuf, pbuf, state, aexp_s, xdt_s, xbc_s, conv_s, y_s, gate_s, mixsc_s):
    tl = x_ref.shape[1]
    half = tl // 2
    width = x_ref.shape[2]
    gcols = width // SSD_GROUPS
    pairs_per_group = gcols // LANES
    x_slabs = width // LANES

    @pl.when(pl.program_id(1) == 0)
    def _():
        xbuf[:, 0:HALO, :] = jnp.zeros((xbuf.shape[0], HALO, LANES), F32)
        pbuf[:, 0:HALO, :] = jnp.zeros((pbuf.shape[0], HALO, LANES), F32)
        state[...] = jnp.zeros(state.shape, F32)

    x = x_ref[0]
    sh1 = mod_ref[0, 0:1, :]
    sc1 = mod_ref[0, 1:2, :]
    g1 = mod_ref[0, 2:3, :]
    ms = jnp.mean(x * x, axis=-1, keepdims=True)
    h = (x * lax.rsqrt(ms + EPS)) * (n1w_ref[...] * (1.0 + sc1)) + sh1
    hb = h.astype(BF16)

    xbc_w = xbuf.shape[0] * LANES
    col_z, col_xbc = (wa_ref, 0), (wa_ref, width)
    col_scb, col_scc, col_sch = (wb_ref, 0), (wb_ref, width), (wb_ref, 2 * width)

    def in_proj(where, ncols, offset=0):
        w_ref, col0 = where
        return _dot(hb, w_ref[:, col0 + offset:col0 + offset + ncols])

    side = {}
    n_fill = width // SIDE_COLS
    slabs_per_fill = SIDE_COLS // LANES

    def fill_sc_conv(k):
        c0 = k * SIDE_COLS
        prod = in_proj(col_scc, SIDE_COLS, c0) * in_proj(col_sch, SIDE_COLS, c0)
        sw = sccw_ref[:, c0:c0 + SIDE_COLS]
        for j, rows in enumerate(_lane_slabs(prod)):
            s = k * slabs_per_fill + j
            taps = [sw[t:t + 1, j * LANES:(j + 1) * LANES] for t in range(sw.shape[0])]
            for parity, acc in enumerate(_causal_conv_slab(pbuf.at[s], rows, taps, None)):
                conv_s[s, pl.ds(parity, half, stride=2), :] = acc

    def fill_sc_gate(k):
        scb = in_proj(col_scb, SIDE_COLS, k * SIDE_COLS)
        for j, rows in enumerate(_lane_slabs(scb)):
            s = k * slabs_per_fill + j
            conv_s[s] = rows * conv_s[s]

    def fill_sc_norm(_):
        y_sc = jnp.concatenate([conv_s[s] for s in range(conv_s.shape[0])], axis=1)
        ssq = _dot((y_sc * y_sc).astype(BF16), gsum_ref[...])
        inv_g = lax.rsqrt(ssq * (1.0 / SC_GROUP) + EPS)
        side["y_scn"] = (y_sc * _dot(_split_bf16(inv_g), expand_ref[:, 0:width])
                         * scnw_ref[...]).astype(BF16)

    def fill_gate(k):
        c0 = k * SIDE_COLS
        gate_s[:, c0:c0 + SIDE_COLS] = _silu(in_proj(col_z, SIDE_COLS, c0))

    def fill_sc_out(k):
        c0 = k * SIDE_COLS
        mixsc_s[:, c0:c0 + SIDE_COLS] = _dot(side["y_scn"], wout_ref[width:, c0:c0 + SIDE_COLS])

    side_work = ([(fill_sc_conv, k) for k in range(n_fill)]
                 + [(fill_sc_gate, k) for k in range(n_fill)]
                 + [(fill_sc_norm, 0)]
                 + [(fill_gate, k) for k in range(n_fill)]
                 + [(fill_sc_out, k) for k in range(n_fill)])

    def run_side(n):
        for _ in range(min(n, len(side_work))):
            fn, k = side_work.pop(0)
            fn(k)

    cw = cw_ref[...]
    cb = cb_ref[...]
    xbc_raw = in_proj(col_xbc, xbc_w)
    dtv = _dot(hb, wdt_ref[...]) + dtb_ref[...]
    run_side(SIDE_BEFORE_CONV)
    for s, rows in enumerate(_lane_slabs(xbc_raw)):
        ls = slice(s * LANES, (s + 1) * LANES)
        taps = [cw[k:k + 1, ls] for k in range(cw.shape[0])]
        for parity, acc in enumerate(_causal_conv_slab(xbuf.at[s], rows, taps, cb[:, ls])):
            xbc_s[s, pl.ds(parity, half, stride=2), :] = _silu(acc)
    xs = jnp.concatenate([xbc_s[s] for s in range(x_slabs)], axis=1)

    run_side(SIDE_BEFORE_SCAN)
    dt = jnp.maximum(dtv, 0.0) + jnp.log1p(jnp.exp(-jnp.abs(dtv)))
    a = dt * (-jnp.exp(alog_ref[...]))
    n_blocks = tl // SCAN_BLOCK
    a_cs_blocks = []
    for blk in range(n_blocks):
        acs2 = _dot(btri_ref[...], _split_bf16(a[blk * SCAN_BLOCK:(blk + 1) * SCAN_BLOCK, :]))
        a_cs_blocks.append(acs2[:, 0:LANES] + acs2[:, LANES:])

    expand = expand_ref[:, 0:width]
    aexp_s[...] = _dot(_split_bf16(jnp.concatenate(a_cs_blocks, axis=0)), expand)
    xdt_s[...] = xs * _dot(_split_bf16(dt), expand)

    bl = SCAN_BLOCK
    causal = (lax.broadcasted_iota(jnp.int32, (bl, bl), 0)
              >= lax.broadcasted_iota(jnp.int32, (bl, bl), 1))
    left_half = lax.broadcasted_iota(jnp.int32, (bl, LANES), 1) < HEAD_DIM
    for blk in range(n_blocks):
        rows = slice(blk * bl, (blk + 1) * bl)
        a_cs = a_cs_blocks[blk]
        acs_t = a_cs.T
        aexp = aexp_s[rows, :]
        last = aexp_s[(blk + 1) * bl - 1:(blk + 1) * bl, :]
        dec_start = jnp.exp(aexp)
        xdt = xdt_s[rows, :]
        xw_b = (xdt * jnp.exp(last - aexp)).astype(BF16)
        block_decay = jnp.exp(last)
        bm_b = [xbc_s[x_slabs + g, rows, :].astype(BF16) for g in range(SSD_GROUPS)]
        cm_b = [xbc_s[x_slabs + SSD_GROUPS + g, rows, :].astype(BF16) for g in range(SSD_GROUPS)]
        prev = state[...]
        prev_b = prev.astype(BF16)
        for g in range(SSD_GROUPS):
            gs = slice(g * gcols, (g + 1) * gcols)
            cb = lax.dot_general(cm_b[g], bm_b[g], (((1,), (1,)), ((), ())),
                                 preferred_element_type=F32)
            y_off = _dot(cm_b[g], prev_b[:, gs])
            s_new = lax.dot_general(bm_b[g], xw_b[:, gs], (((0,), (0,)), ((), ())),
                                    preferred_element_type=F32)
            state[:, gs] = prev[:, gs] * block_decay[:, gs] + s_new
            for hp in range(pairs_per_group):
                p = g * pairs_per_group + hp
                cs = slice(p * LANES, (p + 1) * LANES)
                decay = []
                for h in (2 * p, 2 * p + 1):
                    seg = a_cs[:, h:h + 1] - acs_t[h:h + 1, :]
                    decay.append((jnp.exp(jnp.where(causal, seg, NEG_BIG)) * cb).astype(BF16))
                xd2 = xdt[:, cs]
                rhs = jnp.concatenate([jnp.where(left_half, xd2, 0.0),
                                       jnp.where(left_half, 0.0, xd2)], axis=0).astype(BF16)
                y_diag = _dot(jnp.concatenate(decay, axis=1), rhs)
                y_s[rows, cs] = (y_diag
                                 + y_off[:, hp * LANES:(hp + 1) * LANES] * dec_start[:, cs])
                run_side(SIDE_PER_PAIR)
    run_side(len(side_work))

    y = (y_s[...] + dskip_ref[...] * xs) * gate_s[...]
    parts = []
    for g in range(SSD_GROUPS):
        yg = y[:, g * gcols:(g + 1) * gcols]
        inv = lax.rsqrt(jnp.mean(yg * yg, axis=-1, keepdims=True) + EPS)
        parts.append(yg * inv)
    y_ssd = (jnp.concatenate(parts, axis=1) * ssdnw_ref[...]).astype(BF16)

    mix = _dot(y_ssd, wout_ref[0:width, 0:width]) + mixsc_s[...]
    o_ref[0] = x + g1 * mix


def _mixer(x, mod, n1w, wa, wb, wdt, cw, cb, dtb, alog, dskip, ssdnw, sccw, scnw, wout, btri,
           expand, gsum):
    bsz, seq, d = x.shape
    tl = MIX_TILE
    xbc_w = cw.shape[1]
    consts = (n1w, wa, wb, wdt, cw, cb, dtb, alog, dskip, ssdnw, sccw, scnw, wout, btri, expand,
              gsum)
    return pl.pallas_call(
        _mixer_kernel,
        out_shape=jax.ShapeDtypeStruct((bsz, seq, d), F32),
        grid=(bsz, seq // tl),
        in_specs=[
            pl.BlockSpec((1, tl, d), lambda b, l: (b, l, 0)),
            pl.BlockSpec((1, mod.shape[1], d), lambda b, l: (b, 0, 0)),
        ] + [_resident(a.shape) for a in consts],
        out_specs=pl.BlockSpec((1, tl, d), lambda b, l: (b, l, 0)),
        scratch_shapes=[
            pltpu.VMEM((xbc_w // LANES, tl + HALO, LANES), F32),
            pltpu.VMEM((d // LANES, tl + HALO, LANES), F32),
            pltpu.VMEM((SSD_STATE, d), F32),
            pltpu.VMEM((tl, d), F32),
            pltpu.VMEM((tl, d), F32),
            pltpu.VMEM((xbc_w // LANES, tl, LANES), F32),
            pltpu.VMEM((d // LANES, tl, LANES), F32),
            pltpu.VMEM((tl, d), F32),
            pltpu.VMEM((tl, d), F32),
            pltpu.VMEM((tl, d), F32),
        ],
        compiler_params=pltpu.CompilerParams(
            dimension_semantics=("arbitrary", "arbitrary"),
            vmem_limit_bytes=VMEM_LIMIT_BYTES),
        name="mixer",
    )(x, mod, *consts)


def _ffn_kernel(x_ref, mod_ref, n2w_ref, wup_ref, fcw_ref, fcb_ref, wdown_ref, fnw_ref,
                o_ref, ubuf, act_s, dn_s):
    tl = x_ref.shape[1]
    half = tl // 2
    d = x_ref.shape[2]
    hidden = wdown_ref.shape[0]
    n_col_chunks = hidden // FFN_COLS
    slabs_per_chunk = FFN_COLS // LANES

    @pl.when(pl.program_id(1) == 0)
    def _():
        ubuf[:, 0:HALO, :] = jnp.zeros((ubuf.shape[0], HALO, LANES), F32)

    x = x_ref[0]
    sh2 = mod_ref[0, 3:4, :]
    sc2 = mod_ref[0, 4:5, :]
    g2 = mod_ref[0, 5:6, :]
    ms = jnp.mean(x * x, axis=-1, keepdims=True)
    h = (x * lax.rsqrt(ms + EPS)) * (n2w_ref[...] * (1.0 + sc2)) + sh2
    hb = h.astype(BF16)

    def up_conv(col0):
        up = _dot(hb, wup_ref[:, col0:col0 + FFN_COLS])
        cols = []
        for s, rows in enumerate(_lane_slabs(up)):
            slab = col0 // LANES + s
            ls = slice(slab * LANES, (slab + 1) * LANES)
            w = fcw_ref[:, ls]
            taps = [w[k:k + 1, :] for k in range(w.shape[0])]
            even, odd = _causal_conv_slab(ubuf.at[slab], rows, taps, fcb_ref[:, ls])
            cols.append(jnp.concatenate([even, odd], axis=0))
        return jnp.concatenate(cols, axis=1)

    for j in range(n_col_chunks):
        gate = up_conv(j * FFN_COLS)
        value = up_conv(hidden + j * FFN_COLS)
        act_s[:, j * FFN_COLS:(j + 1) * FFN_COLS] = (_silu(gate) * value).astype(BF16)

    down = _dot(act_s[...], wdown_ref[:, 0:d])
    for s, rows in enumerate(_lane_slabs(down)):
        dn_s[s, pl.ds(0, half, stride=2), :] = rows[0:half, :]
        dn_s[s, pl.ds(1, half, stride=2), :] = rows[half:, :]
    x2 = x + g2 * jnp.concatenate([dn_s[s] for s in range(d // LANES)], axis=1)
    ms2 = jnp.mean(x2 * x2, axis=-1, keepdims=True)
    o_ref[0] = (x2 * lax.rsqrt(ms2 + EPS)) * fnw_ref[...]


def _ffn(x, mod, n2w, wup, fcw, fcb, wdown, fnw):
    bsz, seq, d = x.shape
    tl = FFN_TILE
    hidden = wdown.shape[0]
    consts = (n2w, wup, fcw, fcb, wdown, fnw)
    return pl.pallas_call(
        _ffn_kernel,
        out_shape=jax.ShapeDtypeStruct((bsz, seq, d), F32),
        grid=(bsz, seq // tl),
        in_specs=[
            pl.BlockSpec((1, tl, d), lambda b, l: (b, l, 0)),
            pl.BlockSpec((1, mod.shape[1], d), lambda b, l: (b, 0, 0)),
        ] + [_resident(a.shape) for a in consts],
        out_specs=pl.BlockSpec((1, tl, d), lambda b, l: (b, l, 0)),
        scratch_shapes=[
            pltpu.VMEM((2 * hidden // LANES, tl + HALO, LANES), F32),
            pltpu.VMEM((tl, hidden), BF16),
            pltpu.VMEM((d // LANES, tl, LANES), F32),
        ],
        compiler_params=pltpu.CompilerParams(
            dimension_semantics=("arbitrary", "arbitrary"),
            vmem_limit_bytes=VMEM_LIMIT_BYTES),
        name="ffn",
    )(x, mod, *consts)


def kernel(x, c, w_ada, b_ada, norm1_w, w_in, ssd_conv_w, ssd_conv_b, dt_bias, a_log, d_skip,
           ssd_norm_w, sc_conv_w, sc_norm_w, w_out, norm2_w, w_up, ffn_conv_w, ffn_conv_b, w_down,
           final_norm_w):
    bsz, seq, d = x.shape
    heads = dt_bias.shape[1]
    xbc_w = ssd_conv_w.shape[2]
    hidden = w_down.shape[1]
    assert w_in.shape[0] == 1, "one layer: the final norm is fused into the ffn call"
    assert d % LANES == 0 and seq % FFN_TILE == 0 and seq % MIX_TILE == 0
    assert heads * HEAD_DIM == d and heads <= 2 * SUBLANES and hidden % FFN_COLS == 0
    assert xbc_w == d + 2 * SSD_GROUPS * SSD_STATE

    t = jnp.arange(SCAN_BLOCK)
    btri = (t[:, None] >= t[None, :]).astype(BF16)
    ch = jnp.arange(d)
    j2 = jnp.arange(2 * LANES)
    expand = ((j2[:, None] % LANES) == (ch[None, :] // HEAD_DIM)).astype(BF16)
    gsum = ((ch[:, None] // SC_GROUP) == jnp.arange(LANES)[None, :]).astype(BF16)

    def pad_lanes(v):
        return jnp.pad(v, ((0, 0), (0, LANES - v.shape[1])))

    def widen(w):
        return jnp.pad(w, ((0, 0), (0, LANES))) if w.shape[1] % (8 * LANES) == 0 else w

    mod = _adaln_mod(c, w_ada[0], b_ada[0]).reshape(bsz, 6, d)
    wi = w_in[0]
    dt0 = d + xbc_w
    wa = wi[:, :dt0].astype(BF16)
    wb = widen(wi[:, dt0 + heads:]).astype(BF16)
    wdt = pad_lanes(wi[:, dt0:dt0 + heads]).astype(BF16)
    x1 = _mixer(
        x, mod, norm1_w, wa, wb, wdt, ssd_conv_w[0], ssd_conv_b, pad_lanes(dt_bias),
        pad_lanes(a_log),
        jnp.repeat(d_skip[0], HEAD_DIM)[None], ssd_norm_w, sc_conv_w[0], sc_norm_w,
        widen(w_out[0]).astype(BF16), btri, widen(expand), gsum)
    return _ffn(x1, mod, norm2_w, w_up[0].astype(BF16), ffn_conv_w[0], ffn_conv_b,
                widen(w_down[0]).astype(BF16), final_norm_w[None])
```

```python
import jax
import jax.numpy as jnp
from jax import lax
from jax.experimental import pallas as pl
from jax.experimental.pallas import tpu as pltpu

EPS = 1e-6
HEAD_DIM = 64
SSD_GROUPS = 2
SSD_STATE = 128
SC_GROUP = 64
LANES = 128
SUBLANES = 8
HALO = SUBLANES

MIX_TILE = 512
SCAN_BLOCK = 256
SIDE_COLS = 256
SIDE_BEFORE_CONV, SIDE_BEFORE_SCAN, SIDE_PER_PAIR = 8, 2, 1
FFN_TILE = 512
FFN_COLS = 256

VMEM_LIMIT_BYTES = 56 * 1024 * 1024

BF16 = jnp.bfloat16
F32 = jnp.float32
NEG_BIG = -1e30


def _dot(a, b):
    return jnp.dot(a, b, preferred_element_type=F32)


def _sigmoid(v):
    return 1.0 / (1.0 + jnp.exp(-v))


def _silu(v):
    return v * _sigmoid(v)


def _split_bf16(v):
    hi = v.astype(BF16)
    lo = (v - hi.astype(F32)).astype(BF16)
    return jnp.concatenate([hi, lo], axis=1)


def _resident(shape):
    nd = len(shape)
    return pl.BlockSpec(shape, lambda *_: (0,) * nd, pipeline_mode=pl.Buffered(1))


def _lane_slabs(v):
    return [v[:, s * LANES:(s + 1) * LANES] for s in range(v.shape[1] // LANES)]


def _causal_conv_slab(buf, new_rows, taps, bias):
    tl = new_rows.shape[0]
    n_taps = len(taps)
    buf[HALO:HALO + tl, :] = new_rows
    out = []
    for parity in range(2):
        acc = bias
        for k in range(n_taps):
            rows = pl.ds(HALO - (n_taps - 1) + k + parity, tl // 2, stride=2)
            term = taps[k] * buf[rows, :]
            acc = term if acc is None else acc + term
        out.append(acc)
    buf[0:HALO, :] = buf[tl:tl + HALO, :]
    return out


def _mod_kernel(c_ref, w_ref, b_ref, o_ref):
    c = c_ref[...]
    o_ref[...] = _dot(_silu(c).astype(BF16), w_ref[...].astype(BF16)) + b_ref[...]


def _adaln_mod(c, w_ada, b_ada):
    bsz, d = c.shape
    n = w_ada.shape[1]
    return pl.pallas_call(
        _mod_kernel,
        out_shape=jax.ShapeDtypeStruct((bsz, n), F32),
        grid=(n // d,),
        in_specs=[
            pl.BlockSpec((bsz, d), lambda j: (0, 0)),
            pl.BlockSpec((d, d), lambda j: (0, j)),
            pl.BlockSpec((1, d), lambda j: (0, j)),
        ],
        out_specs=pl.BlockSpec((bsz, d), lambda j: (0, j)),
        compiler_params=pltpu.CompilerParams(dimension_semantics=("arbitrary",)),
        name="adaln_mod",
    )(c, w_ada, b_ada.reshape(1, n))


def _mixer_kernel(x_ref, mod_ref, n1w_ref, wa_ref, wb_ref, wdt_ref, cw_ref, cb_ref, dtb_ref,
                  alog_ref, dskip_ref, ssdnw_ref, sccw_ref, scnw_ref, wout_ref, btri_ref,
                  expand_ref, gsum_ref,
                  o_ref,
                  xbuf, pbuf, state, aexp_s, xdt_s, xbc_s, conv_s, y_s, gate_s, mixsc_s):
    tl = x_ref.shape[1]
    half = tl // 2
    width = x_ref.shape[2]
    gcols = width // SSD_GROUPS
    pairs_per_group = gcols // LANES
    x_slabs = width // LANES

    @pl.when(pl.program_id(1) == 0)
    def _():
        xbuf[:, 0:HALO, :] = jnp.zeros((xbuf.shape[0], HALO, LANES), F32)
        pbuf[:, 0:HALO, :] = jnp.zeros((pbuf.shape[0], HALO, LANES), F32)
        state[...] = jnp.zeros(state.shape, F32)

    x = x_ref[0]
    sh1 = mod_ref[0, 0:1, :]
    sc1 = mod_ref[0, 1:2, :]
    g1 = mod_ref[0, 2:3, :]
    ms = jnp.mean(x * x, axis=-1, keepdims=True)
    h = (x * lax.rsqrt(ms + EPS)) * (n1w_ref[...] * (1.0 + sc1)) + sh1
    hb = h.astype(BF16)

    xbc_w = xbuf.shape[0] * LANES
    col_z, col_xbc = (wa_ref, 0), (wa_ref, width)
    col_scb, col_scc, col_sch = (wb_ref, 0), (wb_ref, width), (wb_ref, 2 * width)

    def in_proj(where, ncols, offset=0):
        w_ref, col0 = where
        return _dot(hb, w_ref[:, col0 + offset:col0 + offset + ncols])

    side = {}
    n_fill = width // SIDE_COLS
    slabs_per_fill = SIDE_COLS // LANES

    def fill_sc_conv(k):
        c0 = k * SIDE_COLS
        prod = in_proj(col_scc, SIDE_COLS, c0) * in_proj(col_sch, SIDE_COLS, c0)
        sw = sccw_ref[:, c0:c0 + SIDE_COLS]
        for j, rows in enumerate(_lane_slabs(prod)):
            s = k * slabs_per_fill + j
            taps = [sw[t:t + 1, j * LANES:(j + 1) * LANES] for t in range(sw.shape[0])]
            for parity, acc in enumerate(_causal_conv_slab(pbuf.at[s], rows, taps, None)):
                conv_s[s, pl.ds(parity, half, stride=2), :] = acc

    def fill_sc_gate(k):
        scb = in_proj(col_scb, SIDE_COLS, k * SIDE_COLS)
        for j, rows in enumerate(_lane_slabs(scb)):
            s = k * slabs_per_fill + j
            conv_s[s] = rows * conv_s[s]

    def fill_sc_norm(_):
        y_sc = jnp.concatenate([conv_s[s] for s in range(conv_s.shape[0])], axis=1)
        ssq = _dot((y_sc * y_sc).astype(BF16), gsum_ref[...])
        inv_g = lax.rsqrt(ssq * (1.0 / SC_GROUP) + EPS)
        side["y_scn"] = (y_sc * _dot(_split_bf16(inv_g), expand_ref[:, 0:width])
                         * scnw_ref[...]).astype(BF16)

    def fill_gate(k):
        c0 = k * SIDE_COLS
        gate_s[:, c0:c0 + SIDE_COLS] = _silu(in_proj(col_z, SIDE_COLS, c0))

    def fill_sc_out(k):
        c0 = k * SIDE_COLS
        mixsc_s[:, c0:c0 + SIDE_COLS] = _dot(side["y_scn"], wout_ref[width:, c0:c0 + SIDE_COLS])

    side_work = ([(fill_sc_conv, k) for k in range(n_fill)]
                 + [(fill_sc_gate, k) for k in range(n_fill)]
                 + [(fill_sc_norm, 0)]
                 + [(fill_gate, k) for k in range(n_fill)]
                 + [(fill_sc_out, k) for k in range(n_fill)])

    def run_side(n):
        for _ in range(min(n, len(side_work))):
            fn, k = side_work.pop(0)
            fn(k)

    cw = cw_ref[...]
    cb = cb_ref[...]
    xbc_raw = in_proj(col_xbc, xbc_w)
    dtv = _dot(hb, wdt_ref[...]) + dtb_ref[...]
    run_side(SIDE_BEFORE_CONV)
    for s, rows in enumerate(_lane_slabs(xbc_raw)):
        ls = slice(s * LANES, (s + 1) * LANES)
        taps = [cw[k:k + 1, ls] for k in range(cw.shape[0])]
        for parity, acc in enumerate(_causal_conv_slab(xbuf.at[s], rows, taps, cb[:, ls])):
            xbc_s[s, pl.ds(parity, half, stride=2), :] = _silu(acc)
    xs = jnp.concatenate([xbc_s[s] for s in range(x_slabs)], axis=1)

    run_side(SIDE_BEFORE_SCAN)
    dt = jnp.maximum(dtv, 0.0) + jnp.log1p(jnp.exp(-jnp.abs(dtv)))
    a = dt * (-jnp.exp(alog_ref[...]))
    n_blocks = tl // SCAN_BLOCK
    a_cs_blocks = []
    for blk in range(n_blocks):
        acs2 = _dot(btri_ref[...], _split_bf16(a[blk * SCAN_BLOCK:(blk + 1) * SCAN_BLOCK, :]))
        a_cs_blocks.append(acs2[:, 0:LANES] + acs2[:, LANES:])

    expand = expand_ref[:, 0:width]
    aexp_s[...] = _dot(_split_bf16(jnp.concatenate(a_cs_blocks, axis=0)), expand)
    xdt_s[...] = xs * _dot(_split_bf16(dt), expand)

    bl = SCAN_BLOCK
    causal_t = (lax.broadcasted_iota(jnp.int32, (bl, bl), 0)
                <= lax.broadcasted_iota(jnp.int32, (bl, bl), 1))
    for blk in range(n_blocks):
        rows = slice(blk * bl, (blk + 1) * bl)
        a_cs = a_cs_blocks[blk]
        acs_t = a_cs.T
        aexp = aexp_s[rows, :]
        last = aexp_s[(blk + 1) * bl - 1:(blk + 1) * bl, :]
        dec_start = jnp.exp(aexp)
        xdt = xdt_s[rows, :]
        xw_b = (xdt * jnp.exp(last - aexp)).astype(BF16)
        block_decay = jnp.exp(last)
        bm_b = [xbc_s[x_slabs + g, rows, :].astype(BF16) for g in range(SSD_GROUPS)]
        cm_b = [xbc_s[x_slabs + SSD_GROUPS + g, rows, :].astype(BF16) for g in range(SSD_GROUPS)]
        prev = state[...]
        prev_b = prev.astype(BF16)
        for g in range(SSD_GROUPS):
            gs = slice(g * gcols, (g + 1) * gcols)
            cb_t = lax.dot_general(bm_b[g], cm_b[g], (((1,), (1,)), ((), ())),
                                   preferred_element_type=F32)
            y_off = _dot(cm_b[g], prev_b[:, gs])
            s_new = lax.dot_general(bm_b[g], xw_b[:, gs], (((0,), (0,)), ((), ())),
                                    preferred_element_type=F32)
            state[:, gs] = prev[:, gs] * block_decay[:, gs] + s_new
            for hp in range(pairs_per_group):
                p = g * pairs_per_group + hp
                cs = slice(p * LANES, (p + 1) * LANES)
                xd_t = xdt[:, cs].T.astype(BF16)
                y_t = []
                for j, h in enumerate((2 * p, 2 * p + 1)):
                    seg_t = acs_t[h:h + 1, :] - a_cs[:, h:h + 1]
                    decay_t = (jnp.exp(jnp.where(causal_t, seg_t, NEG_BIG)) * cb_t).astype(BF16)
                    y_t.append(_dot(xd_t[j * HEAD_DIM:(j + 1) * HEAD_DIM, :], decay_t))
                y_diag = jnp.concatenate(y_t, axis=0).T
                y_s[rows, cs] = (y_diag
                                 + y_off[:, hp * LANES:(hp + 1) * LANES] * dec_start[:, cs])
                run_side(SIDE_PER_PAIR)
    run_side(len(side_work))

    y = (y_s[...] + dskip_ref[...] * xs) * gate_s[...]
    parts = []
    for g in range(SSD_GROUPS):
        yg = y[:, g * gcols:(g + 1) * gcols]
        inv = lax.rsqrt(jnp.mean(yg * yg, axis=-1, keepdims=True) + EPS)
        parts.append(yg * inv)
    y_ssd = (jnp.concatenate(parts, axis=1) * ssdnw_ref[...]).astype(BF16)

    mix = _dot(y_ssd, wout_ref[0:width, 0:width]) + mixsc_s[...]
    o_ref[0] = x + g1 * mix


def _mixer(x, mod, n1w, wa, wb, wdt, cw, cb, dtb, alog, dskip, ssdnw, sccw, scnw, wout, btri,
           expand, gsum):
    bsz, seq, d = x.shape
    tl = MIX_TILE
    xbc_w = cw.shape[1]
    consts = (n1w, wa, wb, wdt, cw, cb, dtb, alog, dskip, ssdnw, sccw, scnw, wout, btri, expand,
              gsum)
    return pl.pallas_call(
        _mixer_kernel,
        out_shape=jax.ShapeDtypeStruct((bsz, seq, d), F32),
        grid=(bsz, seq // tl),
        in_specs=[
            pl.BlockSpec((1, tl, d), lambda b, l: (b, l, 0)),
            pl.BlockSpec((1, mod.shape[1], d), lambda b, l: (b, 0, 0)),
        ] + [_resident(a.shape) for a in consts],
        out_specs=pl.BlockSpec((1, tl, d), lambda b, l: (b, l, 0)),
        scratch_shapes=[
            pltpu.VMEM((xbc_w // LANES, tl + HALO, LANES), F32),
            pltpu.VMEM((d // LANES, tl + HALO, LANES), F32),
            pltpu.VMEM((SSD_STATE, d), F32),
            pltpu.VMEM((tl, d), F32),
            pltpu.VMEM((tl, d), F32),
            pltpu.VMEM((xbc_w // LANES, tl, LANES), F32),
            pltpu.VMEM((d // LANES, tl, LANES), F32),
            pltpu.VMEM((tl, d), F32),
            pltpu.VMEM((tl, d), F32),
            pltpu.VMEM((tl, d), F32),
        ],
        compiler_params=pltpu.CompilerParams(
            dimension_semantics=("arbitrary", "arbitrary"),
            vmem_limit_bytes=VMEM_LIMIT_BYTES),
        name="mixer",
    )(x, mod, *consts)


def _ffn_kernel(x_ref, mod_ref, n2w_ref, wup_ref, fcw_ref, fcb_ref, wdown_ref, fnw_ref,
                o_ref, ubuf, act_s, dn_s):
    tl = x_ref.shape[1]
    half = tl // 2
    d = x_ref.shape[2]
    hidden = wdown_ref.shape[0]
    n_col_chunks = hidden // FFN_COLS
    slabs_per_chunk = FFN_COLS // LANES

    @pl.when(pl.program_id(1) == 0)
    def _():
        ubuf[:, 0:HALO, :] = jnp.zeros((ubuf.shape[0], HALO, LANES), F32)

    x = x_ref[0]
    sh2 = mod_ref[0, 3:4, :]
    sc2 = mod_ref[0, 4:5, :]
    g2 = mod_ref[0, 5:6, :]
    ms = jnp.mean(x * x, axis=-1, keepdims=True)
    h = (x * lax.rsqrt(ms + EPS)) * (n2w_ref[...] * (1.0 + sc2)) + sh2
    hb = h.astype(BF16)

    def up_conv(col0):
        up = _dot(hb, wup_ref[:, col0:col0 + FFN_COLS])
        cols = []
        for s, rows in enumerate(_lane_slabs(up)):
            slab = col0 // LANES + s
            ls = slice(slab * LANES, (slab + 1) * LANES)
            w = fcw_ref[:, ls]
            taps = [w[k:k + 1, :] for k in range(w.shape[0])]
            even, odd = _causal_conv_slab(ubuf.at[slab], rows, taps, fcb_ref[:, ls])
            cols.append(jnp.concatenate([even, odd], axis=0))
        return jnp.concatenate(cols, axis=1)

    for j in range(n_col_chunks):
        gate = up_conv(j * FFN_COLS)
        value = up_conv(hidden + j * FFN_COLS)
        act_s[:, j * FFN_COLS:(j + 1) * FFN_COLS] = (_silu(gate) * value).astype(BF16)

    down = _dot(act_s[...], wdown_ref[:, 0:d])
    for s, rows in enumerate(_lane_slabs(down)):
        dn_s[s, pl.ds(0, half, stride=2), :] = rows[0:half, :]
        dn_s[s, pl.ds(1, half, stride=2), :] = rows[half:, :]
    x2 = x + g2 * jnp.concatenate([dn_s[s] for s in range(d // LANES)], axis=1)
    ms2 = jnp.mean(x2 * x2, axis=-1, keepdims=True)
    o_ref[0] = (x2 * lax.rsqrt(ms2 + EPS)) * fnw_ref[...]


def _ffn(x, mod, n2w, wup, fcw, fcb, wdown, fnw):
    bsz, seq, d = x.shape
    tl = FFN_TILE
    hidden = wdown.shape[0]
    consts = (n2w, wup, fcw, fcb, wdown, fnw)
    return pl.pallas_call(
        _ffn_kernel,
        out_shape=jax.ShapeDtypeStruct((bsz, seq, d), F32),
        grid=(bsz, seq // tl),
        in_specs=[
            pl.BlockSpec((1, tl, d), lambda b, l: (b, l, 0)),
            pl.BlockSpec((1, mod.shape[1], d), lambda b, l: (b, 0, 0)),
        ] + [_resident(a.shape) for a in consts],
        out_specs=pl.BlockSpec((1, tl, d), lambda b, l: (b, l, 0)),
        scratch_shapes=[
            pltpu.VMEM((2 * hidden // LANES, tl + HALO, LANES), F32),
            pltpu.VMEM((tl, hidden), BF16),
            pltpu.VMEM((d // LANES, tl, LANES), F32),
        ],
        compiler_params=pltpu.CompilerParams(
            dimension_semantics=("arbitrary", "arbitrary"),
            vmem_limit_bytes=VMEM_LIMIT_BYTES),
        name="ffn",
    )(x, mod, *consts)


def kernel(x, c, w_ada, b_ada, norm1_w, w_in, ssd_conv_w, ssd_conv_b, dt_bias, a_log, d_skip,
           ssd_norm_w, sc_conv_w, sc_norm_w, w_out, norm2_w, w_up, ffn_conv_w, ffn_conv_b, w_down,
           final_norm_w):
    bsz, seq, d = x.shape
    heads = dt_bias.shape[1]
    xbc_w = ssd_conv_w.shape[2]
    hidden = w_down.shape[1]
    assert w_in.shape[0] == 1, "one layer: the final norm is fused into the ffn call"
    assert d % LANES == 0 and seq % FFN_TILE == 0 and seq % MIX_TILE == 0
    assert heads * HEAD_DIM == d and heads <= 2 * SUBLANES and hidden % FFN_COLS == 0
    assert xbc_w == d + 2 * SSD_GROUPS * SSD_STATE

    t = jnp.arange(SCAN_BLOCK)
    btri = (t[:, None] >= t[None, :]).astype(BF16)
    ch = jnp.arange(d)
    j2 = jnp.arange(2 * LANES)
    expand = ((j2[:, None] % LANES) == (ch[None, :] // HEAD_DIM)).astype(BF16)
    gsum = ((ch[:, None] // SC_GROUP) == jnp.arange(LANES)[None, :]).astype(BF16)

    def pad_lanes(v):
        return jnp.pad(v, ((0, 0), (0, LANES - v.shape[1])))

    mod = _adaln_mod(c, w_ada[0], b_ada[0]).reshape(bsz, 6, d)
    def widen(w):
        return jnp.pad(w, ((0, 0), (0, LANES))) if w.shape[1] % (8 * LANES) == 0 else w

    wi = w_in[0]
    dt0 = d + xbc_w
    wa = wi[:, :dt0].astype(BF16)
    wb = widen(wi[:, dt0 + heads:]).astype(BF16)
    wdt = pad_lanes(wi[:, dt0:dt0 + heads]).astype(BF16)
    x1 = _mixer(
        x, mod, norm1_w, wa, wb, wdt, ssd_conv_w[0], ssd_conv_b, pad_lanes(dt_bias),
        pad_lanes(a_log),
        jnp.repeat(d_skip[0], HEAD_DIM)[None], ssd_norm_w, sc_conv_w[0], sc_norm_w,
        widen(w_out[0]).astype(BF16), btri, widen(expand), gsum)
    return _ffn(x1, mod, norm2_w, w_up[0].astype(BF16), ffn_conv_w[0], ffn_conv_b,
                widen(w_down[0]).astype(BF16), final_norm_w[None])
```

```python
import jax
import jax.numpy as jnp
from jax import lax
from jax.experimental import pallas as pl
from jax.experimental.pallas import tpu as pltpu

EPS = 1e-6
HEAD_DIM = 64
SSD_GROUPS = 2
SSD_STATE = 128
SC_GROUP = 64
LANES = 128
SUBLANES = 8
HALO = SUBLANES

MIX_TILE = 512
SCAN_BLOCK = 256
SIDE_COLS = 256
SIDE_BEFORE_CONV, SIDE_BEFORE_SCAN, SIDE_PER_PAIR = 8, 2, 1
FFN_TILE = 512
FFN_COLS = 256
FFN_SUB = 2

VMEM_LIMIT_BYTES = 56 * 1024 * 1024

BF16 = jnp.bfloat16
F32 = jnp.float32
NEG_BIG = -1e30


def _dot(a, b):
    return jnp.dot(a, b, preferred_element_type=F32)


def _sigmoid(v):
    return 1.0 / (1.0 + jnp.exp(-v))


def _silu(v):
    return v * _sigmoid(v)


def _split_bf16(v):
    hi = v.astype(BF16)
    lo = (v - hi.astype(F32)).astype(BF16)
    return jnp.concatenate([hi, lo], axis=1)


def _resident(shape):
    nd = len(shape)
    return pl.BlockSpec(shape, lambda *_: (0,) * nd, pipeline_mode=pl.Buffered(1))


def _lane_slabs(v):
    return [v[:, s * LANES:(s + 1) * LANES] for s in range(v.shape[1] // LANES)]


def _causal_conv_slab(buf, new_rows, taps, bias, n_sub=1):
    tl = new_rows.shape[0]
    sub = tl // n_sub
    n_taps = len(taps)
    buf[HALO:HALO + tl, :] = new_rows
    out = []
    for i in range(n_sub):
        for parity in range(2):
            acc = bias
            for k in range(n_taps):
                rows = pl.ds(HALO - (n_taps - 1) + k + parity + i * sub, sub // 2, stride=2)
                term = taps[k] * buf[rows, :]
                acc = term if acc is None else acc + term
            out.append(acc)
    buf[0:HALO, :] = buf[tl:tl + HALO, :]
    return out


def _mod_kernel(c_ref, w_ref, b_ref, o_ref):
    c = c_ref[...]
    o_ref[...] = _dot(_silu(c).astype(BF16), w_ref[...].astype(BF16)) + b_ref[...]


def _adaln_mod(c, w_ada, b_ada):
    bsz, d = c.shape
    n = w_ada.shape[1]
    return pl.pallas_call(
        _mod_kernel,
        out_shape=jax.ShapeDtypeStruct((bsz, n), F32),
        grid=(n // d,),
        in_specs=[
            pl.BlockSpec((bsz, d), lambda j: (0, 0)),
            pl.BlockSpec((d, d), lambda j: (0, j)),
            pl.BlockSpec((1, d), lambda j: (0, j)),
        ],
        out_specs=pl.BlockSpec((bsz, d), lambda j: (0, j)),
        compiler_params=pltpu.CompilerParams(dimension_semantics=("arbitrary",)),
        name="adaln_mod",
    )(c, w_ada, b_ada.reshape(1, n))


def _mixer_kernel(x_ref, mod_ref, n1w_ref, wa_ref, wb_ref, wdt_ref, cw_ref, cb_ref, dtb_ref,
                  alog_ref, dskip_ref, ssdnw_ref, sccw_ref, scnw_ref, wout_ref, btri_ref,
                  expand_ref, gsum_ref,
                  o_ref,
                  xbuf, pbuf, state, aexp_s, xdt_s, xbc_s, conv_s, y_s, gate_s, mixsc_s):
    tl = x_ref.shape[1]
    half = tl // 2
    width = x_ref.shape[2]
    gcols = width // SSD_GROUPS
    pairs_per_group = gcols // LANES
    x_slabs = width // LANES

    @pl.when(pl.program_id(1) == 0)
    def _():
        xbuf[:, 0:HALO, :] = jnp.zeros((xbuf.shape[0], HALO, LANES), F32)
        pbuf[:, 0:HALO, :] = jnp.zeros((pbuf.shape[0], HALO, LANES), F32)
        state[...] = jnp.zeros(state.shape, F32)

    x = x_ref[0]
    sh1 = mod_ref[0, 0:1, :]
    sc1 = mod_ref[0, 1:2, :]
    g1 = mod_ref[0, 2:3, :]
    ms = jnp.mean(x * x, axis=-1, keepdims=True)
    h = (x * lax.rsqrt(ms + EPS)) * (n1w_ref[...] * (1.0 + sc1)) + sh1
    hb = h.astype(BF16)

    xbc_w = xbuf.shape[0] * LANES
    col_z, col_xbc = (wa_ref, 0), (wa_ref, width)
    col_scb, col_scc, col_sch = (wb_ref, 0), (wb_ref, width), (wb_ref, 2 * width)

    def in_proj(where, ncols, offset=0):
        w_ref, col0 = where
        return _dot(hb, w_ref[:, col0 + offset:col0 + offset + ncols])

    side = {}
    n_fill = width // SIDE_COLS
    slabs_per_fill = SIDE_COLS // LANES

    def fill_sc_conv(k):
        c0 = k * SIDE_COLS
        prod = in_proj(col_scc, SIDE_COLS, c0) * in_proj(col_sch, SIDE_COLS, c0)
        sw = sccw_ref[:, c0:c0 + SIDE_COLS]
        for j, rows in enumerate(_lane_slabs(prod)):
            s = k * slabs_per_fill + j
            taps = [sw[t:t + 1, j * LANES:(j + 1) * LANES] for t in range(sw.shape[0])]
            for parity, acc in enumerate(_causal_conv_slab(pbuf.at[s], rows, taps, None)):
                conv_s[s, pl.ds(parity, half, stride=2), :] = acc

    def fill_sc_gate(k):
        scb = in_proj(col_scb, SIDE_COLS, k * SIDE_COLS)
        for j, rows in enumerate(_lane_slabs(scb)):
            s = k * slabs_per_fill + j
            conv_s[s] = rows * conv_s[s]

    def fill_sc_norm(_):
        y_sc = jnp.concatenate([conv_s[s] for s in range(conv_s.shape[0])], axis=1)
        ssq = _dot((y_sc * y_sc).astype(BF16), gsum_ref[...])
        inv_g = lax.rsqrt(ssq * (1.0 / SC_GROUP) + EPS)
        side["y_scn"] = (y_sc * _dot(_split_bf16(inv_g), expand_ref[:, 0:width])
                         * scnw_ref[...]).astype(BF16)

    def fill_gate(k):
        c0 = k * SIDE_COLS
        gate_s[:, c0:c0 + SIDE_COLS] = _silu(in_proj(col_z, SIDE_COLS, c0))

    def fill_sc_out(k):
        c0 = k * SIDE_COLS
        mixsc_s[:, c0:c0 + SIDE_COLS] = _dot(side["y_scn"], wout_ref[width:, c0:c0 + SIDE_COLS])

    side_work = ([(fill_sc_conv, k) for k in range(n_fill)]
                 + [(fill_sc_gate, k) for k in range(n_fill)]
                 + [(fill_sc_norm, 0)]
                 + [(fill_gate, k) for k in range(n_fill)]
                 + [(fill_sc_out, k) for k in range(n_fill)])

    def run_side(n):
        for _ in range(min(n, len(side_work))):
            fn, k = side_work.pop(0)
            fn(k)

    cw = cw_ref[...]
    cb = cb_ref[...]
    xbc_raw = in_proj(col_xbc, xbc_w)
    dtv = _dot(hb, wdt_ref[...]) + dtb_ref[...]
    run_side(SIDE_BEFORE_CONV)
    for s, rows in enumerate(_lane_slabs(xbc_raw)):
        ls = slice(s * LANES, (s + 1) * LANES)
        taps = [cw[k:k + 1, ls] for k in range(cw.shape[0])]
        for parity, acc in enumerate(_causal_conv_slab(xbuf.at[s], rows, taps, cb[:, ls])):
            xbc_s[s, pl.ds(parity, half, stride=2), :] = _silu(acc)
    xs = jnp.concatenate([xbc_s[s] for s in range(x_slabs)], axis=1)

    run_side(SIDE_BEFORE_SCAN)
    dt = jnp.maximum(dtv, 0.0) + jnp.log1p(jnp.exp(-jnp.abs(dtv)))
    a = dt * (-jnp.exp(alog_ref[...]))
    n_blocks = tl // SCAN_BLOCK
    a_cs_blocks = []
    for blk in range(n_blocks):
        acs2 = _dot(btri_ref[...], _split_bf16(a[blk * SCAN_BLOCK:(blk + 1) * SCAN_BLOCK, :]))
        a_cs_blocks.append(acs2[:, 0:LANES] + acs2[:, LANES:])

    expand = expand_ref[:, 0:width]
    aexp_s[...] = _dot(_split_bf16(jnp.concatenate(a_cs_blocks, axis=0)), expand)
    xdt_s[...] = xs * _dot(_split_bf16(dt), expand)

    bl = SCAN_BLOCK
    causal_t = (lax.broadcasted_iota(jnp.int32, (bl, bl), 0)
                <= lax.broadcasted_iota(jnp.int32, (bl, bl), 1))
    for blk in range(n_blocks):
        rows = slice(blk * bl, (blk + 1) * bl)
        a_cs = a_cs_blocks[blk]
        acs_t = a_cs.T
        aexp = aexp_s[rows, :]
        last = aexp_s[(blk + 1) * bl - 1:(blk + 1) * bl, :]
        dec_start = jnp.exp(aexp)
        xdt = xdt_s[rows, :]
        xw_b = (xdt * jnp.exp(last - aexp)).astype(BF16)
        block_decay = jnp.exp(last)
        bm_b = [xbc_s[x_slabs + g, rows, :].astype(BF16) for g in range(SSD_GROUPS)]
        cm_b = [xbc_s[x_slabs + SSD_GROUPS + g, rows, :].astype(BF16) for g in range(SSD_GROUPS)]
        prev = state[...]
        prev_b = prev.astype(BF16)
        for g in range(SSD_GROUPS):
            gs = slice(g * gcols, (g + 1) * gcols)
            cb_t = lax.dot_general(bm_b[g], cm_b[g], (((1,), (1,)), ((), ())),
                                   preferred_element_type=F32)
            y_off = _dot(cm_b[g], prev_b[:, gs])
            s_new = lax.dot_general(bm_b[g], xw_b[:, gs], (((0,), (0,)), ((), ())),
                                    preferred_element_type=F32)
            state[:, gs] = prev[:, gs] * block_decay[:, gs] + s_new
            for hp in range(pairs_per_group):
                p = g * pairs_per_group + hp
                cs = slice(p * LANES, (p + 1) * LANES)
                xd_t = xdt[:, cs].T.astype(BF16)
                y_t = []
                for j, h in enumerate((2 * p, 2 * p + 1)):
                    seg_t = acs_t[h:h + 1, :] - a_cs[:, h:h + 1]
                    decay_t = (jnp.exp(jnp.where(causal_t, seg_t, NEG_BIG)) * cb_t).astype(BF16)
                    y_t.append(_dot(xd_t[j * HEAD_DIM:(j + 1) * HEAD_DIM, :], decay_t))
                y_diag = jnp.concatenate(y_t, axis=0).T
                y_s[rows, cs] = (y_diag
                                 + y_off[:, hp * LANES:(hp + 1) * LANES] * dec_start[:, cs])
                run_side(SIDE_PER_PAIR)
    run_side(len(side_work))

    y = (y_s[...] + dskip_ref[...] * xs) * gate_s[...]
    parts = []
    for g in range(SSD_GROUPS):
        yg = y[:, g * gcols:(g + 1) * gcols]
        inv = lax.rsqrt(jnp.mean(yg * yg, axis=-1, keepdims=True) + EPS)
        parts.append(yg * inv)
    y_ssd = (jnp.concatenate(parts, axis=1) * ssdnw_ref[...]).astype(BF16)

    mix = _dot(y_ssd, wout_ref[0:width, 0:width]) + mixsc_s[...]
    o_ref[0] = x + g1 * mix


def _mixer(x, mod, n1w, wa, wb, wdt, cw, cb, dtb, alog, dskip, ssdnw, sccw, scnw, wout, btri,
           expand, gsum):
    bsz, seq, d = x.shape
    tl = MIX_TILE
    xbc_w = cw.shape[1]
    consts = (n1w, wa, wb, wdt, cw, cb, dtb, alog, dskip, ssdnw, sccw, scnw, wout, btri, expand,
              gsum)
    return pl.pallas_call(
        _mixer_kernel,
        out_shape=jax.ShapeDtypeStruct((bsz, seq, d), F32),
        grid=(bsz, seq // tl),
        in_specs=[
            pl.BlockSpec((1, tl, d), lambda b, l: (b, l, 0)),
            pl.BlockSpec((1, mod.shape[1], d), lambda b, l: (b, 0, 0)),
        ] + [_resident(a.shape) for a in consts],
        out_specs=pl.BlockSpec((1, tl, d), lambda b, l: (b, l, 0)),
        scratch_shapes=[
            pltpu.VMEM((xbc_w // LANES, tl + HALO, LANES), F32),
            pltpu.VMEM((d // LANES, tl + HALO, LANES), F32),
            pltpu.VMEM((SSD_STATE, d), F32),
            pltpu.VMEM((tl, d), F32),
            pltpu.VMEM((tl, d), F32),
            pltpu.VMEM((xbc_w // LANES, tl, LANES), F32),
            pltpu.VMEM((d // LANES, tl, LANES), F32),
            pltpu.VMEM((tl, d), F32),
            pltpu.VMEM((tl, d), F32),
            pltpu.VMEM((tl, d), F32),
        ],
        compiler_params=pltpu.CompilerParams(
            dimension_semantics=("arbitrary", "arbitrary"),
            vmem_limit_bytes=VMEM_LIMIT_BYTES),
        name="mixer",
    )(x, mod, *consts)


def _ffn_kernel(x_ref, mod_ref, n2w_ref, wup_ref, fcw_ref, fcb_ref, wdown_ref, fnw_ref,
                o_ref, ubuf, act_s, dn_s):
    tl = x_ref.shape[1]
    d = x_ref.shape[2]
    hidden = wdown_ref.shape[0]
    n_col_chunks = hidden // FFN_COLS

    @pl.when(pl.program_id(1) == 0)
    def _():
        ubuf[:, 0:HALO, :] = jnp.zeros((ubuf.shape[0], HALO, LANES), F32)

    x = x_ref[0]
    sh2 = mod_ref[0, 3:4, :]
    sc2 = mod_ref[0, 4:5, :]
    g2 = mod_ref[0, 5:6, :]
    ms = jnp.mean(x * x, axis=-1, keepdims=True)
    h = (x * lax.rsqrt(ms + EPS)) * (n2w_ref[...] * (1.0 + sc2)) + sh2
    hb = h.astype(BF16)

    def up_conv(col0):
        up = _dot(hb, wup_ref[:, col0:col0 + FFN_COLS])
        cols = []
        for s, rows in enumerate(_lane_slabs(up)):
            slab = col0 // LANES + s
            ls = slice(slab * LANES, (slab + 1) * LANES)
            w = fcw_ref[:, ls]
            taps = [w[k:k + 1, :] for k in range(w.shape[0])]
            cols.append(jnp.concatenate(
                _causal_conv_slab(ubuf.at[slab], rows, taps, fcb_ref[:, ls], FFN_SUB), axis=0))
        return jnp.concatenate(cols, axis=1)

    for j in range(n_col_chunks):
        gate = up_conv(j * FFN_COLS)
        value = up_conv(hidden + j * FFN_COLS)
        act_s[:, j * FFN_COLS:(j + 1) * FFN_COLS] = (_silu(gate) * value).astype(BF16)

    sub = tl // FFN_SUB
    for i in range(FFN_SUB):
        rs = slice(i * sub, (i + 1) * sub)
        down = _dot(act_s[rs, :], wdown_ref[:, 0:d])
        for s, rows in enumerate(_lane_slabs(down)):
            dn_s[s, pl.ds(i * sub, sub // 2, stride=2), :] = rows[0:sub // 2, :]
            dn_s[s, pl.ds(i * sub + 1, sub // 2, stride=2), :] = rows[sub // 2:, :]
        x2 = x[rs, :] + g2 * jnp.concatenate([dn_s[s, rs, :] for s in range(d // LANES)], axis=1)
        ms2 = jnp.mean(x2 * x2, axis=-1, keepdims=True)
        o_ref[0, rs, :] = (x2 * lax.rsqrt(ms2 + EPS)) * fnw_ref[...]


def _ffn(x, mod, n2w, wup, fcw, fcb, wdown, fnw):
    bsz, seq, d = x.shape
    tl = FFN_TILE
    hidden = wdown.shape[0]
    consts = (n2w, wup, fcw, fcb, wdown, fnw)
    return pl.pallas_call(
        _ffn_kernel,
        out_shape=jax.ShapeDtypeStruct((bsz, seq, d), F32),
        grid=(bsz, seq // tl),
        in_specs=[
            pl.BlockSpec((1, tl, d), lambda b, l: (b, l, 0)),
            pl.BlockSpec((1, mod.shape[1], d), lambda b, l: (b, 0, 0)),
        ] + [_resident(a.shape) for a in consts],
        out_specs=pl.BlockSpec((1, tl, d), lambda b, l: (b, l, 0)),
        scratch_shapes=[
            pltpu.VMEM((2 * hidden // LANES, tl + HALO, LANES), F32),
            pltpu.VMEM((tl, hidden), BF16),
            pltpu.VMEM((d // LANES, tl, LANES), F32),
        ],
        compiler_params=pltpu.CompilerParams(
            dimension_semantics=("arbitrary", "arbitrary"),
            vmem_limit_bytes=VMEM_LIMIT_BYTES),
        name="ffn",
    )(x, mod, *consts)


def kernel(x, c, w_ada, b_ada, norm1_w, w_in, ssd_conv_w, ssd_conv_b, dt_bias, a_log, d_skip,
           ssd_norm_w, sc_conv_w, sc_norm_w, w_out, norm2_w, w_up, ffn_conv_w, ffn_conv_b, w_down,
           final_norm_w):
    bsz, seq, d = x.shape
    heads = dt_bias.shape[1]
    xbc_w = ssd_conv_w.shape[2]
    hidden = w_down.shape[1]
    assert w_in.shape[0] == 1, "one layer: the final norm is fused into the ffn call"
    assert d % LANES == 0 and seq % FFN_TILE == 0 and seq % MIX_TILE == 0
    assert heads * HEAD_DIM == d and heads <= 2 * SUBLANES and hidden % FFN_COLS == 0
    assert xbc_w == d + 2 * SSD_GROUPS * SSD_STATE

    t = jnp.arange(SCAN_BLOCK)
    btri = (t[:, None] >= t[None, :]).astype(BF16)
    ch = jnp.arange(d)
    j2 = jnp.arange(2 * LANES)
    expand = ((j2[:, None] % LANES) == (ch[None, :] // HEAD_DIM)).astype(BF16)
    gsum = ((ch[:, None] // SC_GROUP) == jnp.arange(LANES)[None, :]).astype(BF16)

    def pad_lanes(v):
        return jnp.pad(v, ((0, 0), (0, LANES - v.shape[1])))

    mod = _adaln_mod(c, w_ada[0], b_ada[0]).reshape(bsz, 6, d)
    def widen(w):
        return jnp.pad(w, ((0, 0), (0, LANES))) if w.shape[1] % (8 * LANES) == 0 else w

    wi = w_in[0]
    dt0 = d + xbc_w
    wa = wi[:, :dt0].astype(BF16)
    wb = widen(wi[:, dt0 + heads:]).astype(BF16)
    wdt = pad_lanes(wi[:, dt0:dt0 + heads]).astype(BF16)
    x1 = _mixer(
        x, mod, norm1_w, wa, wb, wdt, ssd_conv_w[0], ssd_conv_b, pad_lanes(dt_bias),
        pad_lanes(a_log),
        jnp.repeat(d_skip[0], HEAD_DIM)[None], ssd_norm_w, sc_conv_w[0], sc_norm_w,
        widen(w_out[0]).astype(BF16), btri, widen(expand), gsum)
    return _ffn(x1, mod, norm2_w, w_up[0].astype(BF16), ffn_conv_w[0], ffn_conv_b,
                widen(w_down[0]).astype(BF16), final_norm_w[None])
```

```python
import jax
import jax.numpy as jnp
from jax import lax
from jax.experimental import pallas as pl
from jax.experimental.pallas import tpu as pltpu

EPS = 1e-6
HEAD_DIM = 64
SSD_GROUPS = 2
SSD_STATE = 128
SC_GROUP = 64
LANES = 128
SUBLANES = 8
HALO = SUBLANES

MIX_TILE = 512
SCAN_BLOCK = 256
SIDE_COLS = 256
SIDE_BEFORE_CONV, SIDE_BEFORE_SCAN, SIDE_PER_PAIR = 8, 2, 1
FFN_TILE = 512
FFN_COLS = 256
FFN_SUB = 2

VMEM_LIMIT_BYTES = 56 * 1024 * 1024

BF16 = jnp.bfloat16
F32 = jnp.float32
NEG_BIG = -1e30
LOG2E = 1.4426950408889634


def _dot(a, b):
    return jnp.dot(a, b, preferred_element_type=F32)


def _sigmoid(v):
    return 1.0 / (1.0 + jnp.exp(-v))


def _silu(v):
    return v * _sigmoid(v)


def _split_bf16(v):
    hi = v.astype(BF16)
    lo = (v - hi.astype(F32)).astype(BF16)
    return jnp.concatenate([hi, lo], axis=1)


def _resident(shape):
    nd = len(shape)
    return pl.BlockSpec(shape, lambda *_: (0,) * nd, pipeline_mode=pl.Buffered(1))


def _lane_slabs(v):
    return [v[:, s * LANES:(s + 1) * LANES] for s in range(v.shape[1] // LANES)]


def _causal_conv_slab(buf, new_rows, taps, bias, n_sub=1):
    tl = new_rows.shape[0]
    sub = tl // n_sub
    n_taps = len(taps)
    buf[HALO:HALO + tl, :] = new_rows
    out = []
    for i in range(n_sub):
        for parity in range(2):
            acc = bias
            for k in range(n_taps):
                rows = pl.ds(HALO - (n_taps - 1) + k + parity + i * sub, sub // 2, stride=2)
                term = taps[k] * buf[rows, :]
                acc = term if acc is None else acc + term
            out.append(acc)
    buf[0:HALO, :] = buf[tl:tl + HALO, :]
    return out


def _mod_kernel(c_ref, w_ref, b_ref, o_ref):
    c = c_ref[...]
    o_ref[...] = _dot(_silu(c).astype(BF16), w_ref[...].astype(BF16)) + b_ref[...]


def _adaln_mod(c, w_ada, b_ada):
    bsz, d = c.shape
    n = w_ada.shape[1]
    return pl.pallas_call(
        _mod_kernel,
        out_shape=jax.ShapeDtypeStruct((bsz, n), F32),
        grid=(n // d,),
        in_specs=[
            pl.BlockSpec((bsz, d), lambda j: (0, 0)),
            pl.BlockSpec((d, d), lambda j: (0, j)),
            pl.BlockSpec((1, d), lambda j: (0, j)),
        ],
        out_specs=pl.BlockSpec((bsz, d), lambda j: (0, j)),
        compiler_params=pltpu.CompilerParams(dimension_semantics=("arbitrary",)),
        name="adaln_mod",
    )(c, w_ada, b_ada.reshape(1, n))


def _mixer_kernel(x_ref, mod_ref, n1w_ref, wa_ref, wb_ref, wdt_ref, cw_ref, cb_ref, dtb_ref,
                  alog_ref, dskip_ref, ssdnw_ref, sccw_ref, scnw_ref, wout_ref, btri_ref,
                  expand_ref, gsum_ref,
                  o_ref,
                  xbuf, pbuf, state, aexp_s, xdt_s, xbc_s, conv_s, y_s, gate_s, mixsc_s):
    tl = x_ref.shape[1]
    half = tl // 2
    width = x_ref.shape[2]
    gcols = width // SSD_GROUPS
    pairs_per_group = gcols // LANES
    x_slabs = width // LANES

    @pl.when(pl.program_id(1) == 0)
    def _():
        xbuf[:, 0:HALO, :] = jnp.zeros((xbuf.shape[0], HALO, LANES), F32)
        pbuf[:, 0:HALO, :] = jnp.zeros((pbuf.shape[0], HALO, LANES), F32)
        state[...] = jnp.zeros(state.shape, F32)

    x = x_ref[0]
    sh1 = mod_ref[0, 0:1, :]
    sc1 = mod_ref[0, 1:2, :]
    g1 = mod_ref[0, 2:3, :]
    ms = jnp.mean(x * x, axis=-1, keepdims=True)
    h = (x * lax.rsqrt(ms + EPS)) * (n1w_ref[...] * (1.0 + sc1)) + sh1
    hb = h.astype(BF16)

    xbc_w = xbuf.shape[0] * LANES
    col_z, col_xbc = (wa_ref, 0), (wa_ref, width)
    col_scb, col_scc, col_sch = (wb_ref, 0), (wb_ref, width), (wb_ref, 2 * width)

    def in_proj(where, ncols, offset=0):
        w_ref, col0 = where
        return _dot(hb, w_ref[:, col0 + offset:col0 + offset + ncols])

    side = {}
    n_fill = width // SIDE_COLS
    slabs_per_fill = SIDE_COLS // LANES

    def fill_sc_conv(k):
        c0 = k * SIDE_COLS
        prod = in_proj(col_scc, SIDE_COLS, c0) * in_proj(col_sch, SIDE_COLS, c0)
        sw = sccw_ref[:, c0:c0 + SIDE_COLS]
        for j, rows in enumerate(_lane_slabs(prod)):
            s = k * slabs_per_fill + j
            taps = [sw[t:t + 1, j * LANES:(j + 1) * LANES] for t in range(sw.shape[0])]
            for parity, acc in enumerate(_causal_conv_slab(pbuf.at[s], rows, taps, None)):
                conv_s[s, pl.ds(parity, half, stride=2), :] = acc

    def fill_sc_gate(k):
        scb = in_proj(col_scb, SIDE_COLS, k * SIDE_COLS)
        for j, rows in enumerate(_lane_slabs(scb)):
            s = k * slabs_per_fill + j
            conv_s[s] = rows * conv_s[s]

    def fill_sc_norm(_):
        y_sc = jnp.concatenate([conv_s[s] for s in range(conv_s.shape[0])], axis=1)
        ssq = _dot((y_sc * y_sc).astype(BF16), gsum_ref[...])
        inv_g = lax.rsqrt(ssq * (1.0 / SC_GROUP) + EPS)
        side["y_scn"] = (y_sc * _dot(_split_bf16(inv_g), expand_ref[:, 0:width])
                         * scnw_ref[...]).astype(BF16)

    def fill_gate(k):
        c0 = k * SIDE_COLS
        gate_s[:, c0:c0 + SIDE_COLS] = _silu(in_proj(col_z, SIDE_COLS, c0))

    def fill_sc_out(k):
        c0 = k * SIDE_COLS
        mixsc_s[:, c0:c0 + SIDE_COLS] = _dot(side["y_scn"], wout_ref[width:, c0:c0 + SIDE_COLS])

    side_work = ([(fill_sc_conv, k) for k in range(n_fill)]
                 + [(fill_sc_gate, k) for k in range(n_fill)]
                 + [(fill_sc_norm, 0)]
                 + [(fill_gate, k) for k in range(n_fill)]
                 + [(fill_sc_out, k) for k in range(n_fill)])

    def run_side(n):
        for _ in range(min(n, len(side_work))):
            fn, k = side_work.pop(0)
            fn(k)

    cw = cw_ref[...]
    cb = cb_ref[...]
    xbc_raw = in_proj(col_xbc, xbc_w)
    dtv = _dot(hb, wdt_ref[...]) + dtb_ref[...]
    run_side(SIDE_BEFORE_CONV)
    for s, rows in enumerate(_lane_slabs(xbc_raw)):
        ls = slice(s * LANES, (s + 1) * LANES)
        taps = [cw[k:k + 1, ls] for k in range(cw.shape[0])]
        for parity, acc in enumerate(_causal_conv_slab(xbuf.at[s], rows, taps, cb[:, ls])):
            xbc_s[s, pl.ds(parity, half, stride=2), :] = _silu(acc)
    xs = jnp.concatenate([xbc_s[s] for s in range(x_slabs)], axis=1)

    run_side(SIDE_BEFORE_SCAN)
    dt = jnp.maximum(dtv, 0.0) + jnp.log1p(jnp.exp(-jnp.abs(dtv)))
    a = dt * (-jnp.exp(alog_ref[...]))
    n_blocks = tl // SCAN_BLOCK
    a_cs_blocks = []
    for blk in range(n_blocks):
        acs2 = _dot(btri_ref[...], _split_bf16(a[blk * SCAN_BLOCK:(blk + 1) * SCAN_BLOCK, :]))
        a_cs_blocks.append((acs2[:, 0:LANES] + acs2[:, LANES:]) * LOG2E)

    expand = expand_ref[:, 0:width]
    aexp_s[...] = _dot(_split_bf16(jnp.concatenate(a_cs_blocks, axis=0)), expand)
    xdt_s[...] = xs * _dot(_split_bf16(dt), expand)

    bl = SCAN_BLOCK
    causal_t = (lax.broadcasted_iota(jnp.int32, (bl, bl), 0)
                <= lax.broadcasted_iota(jnp.int32, (bl, bl), 1))
    for blk in range(n_blocks):
        rows = slice(blk * bl, (blk + 1) * bl)
        a_cs = a_cs_blocks[blk]
        acs_t = a_cs.T
        aexp = aexp_s[rows, :]
        last = aexp_s[(blk + 1) * bl - 1:(blk + 1) * bl, :]
        dec_start = jnp.exp2(aexp)
        xdt = xdt_s[rows, :]
        xw_b = (xdt * jnp.exp2(last - aexp)).astype(BF16)
        block_decay = jnp.exp2(last)
        bm_b = [xbc_s[x_slabs + g, rows, :].astype(BF16) for g in range(SSD_GROUPS)]
        cm_b = [xbc_s[x_slabs + SSD_GROUPS + g, rows, :].astype(BF16) for g in range(SSD_GROUPS)]
        prev = state[...]
        prev_b = prev.astype(BF16)
        for g in range(SSD_GROUPS):
            gs = slice(g * gcols, (g + 1) * gcols)
            cb_t = lax.dot_general(bm_b[g], cm_b[g], (((1,), (1,)), ((), ())),
                                   preferred_element_type=F32)
            cb_tb = cb_t.astype(BF16)
            y_off = _dot(cm_b[g], prev_b[:, gs])
            s_new = lax.dot_general(bm_b[g], xw_b[:, gs], (((0,), (0,)), ((), ())),
                                    preferred_element_type=F32)
            state[:, gs] = prev[:, gs] * block_decay[:, gs] + s_new
            for hp in range(pairs_per_group):
                p = g * pairs_per_group + hp
                cs = slice(p * LANES, (p + 1) * LANES)
                xd_t = xdt[:, cs].T.astype(BF16)
                y_t = []
                for j, h in enumerate((2 * p, 2 * p + 1)):
                    seg_t = acs_t[h:h + 1, :] - a_cs[:, h:h + 1]
                    decay_t = jnp.exp2(jnp.where(causal_t, seg_t, NEG_BIG)).astype(BF16) * cb_tb
                    y_t.append(_dot(xd_t[j * HEAD_DIM:(j + 1) * HEAD_DIM, :], decay_t))
                y_diag = jnp.concatenate(y_t, axis=0).T
                y_s[rows, cs] = (y_diag
                                 + y_off[:, hp * LANES:(hp + 1) * LANES] * dec_start[:, cs])
                run_side(SIDE_PER_PAIR)
    run_side(len(side_work))

    y = (y_s[...] + dskip_ref[...] * xs) * gate_s[...]
    parts = []
    for g in range(SSD_GROUPS):
        yg = y[:, g * gcols:(g + 1) * gcols]
        inv = lax.rsqrt(jnp.mean(yg * yg, axis=-1, keepdims=True) + EPS)
        parts.append(yg * inv)
    y_ssd = (jnp.concatenate(parts, axis=1) * ssdnw_ref[...]).astype(BF16)

    mix = _dot(y_ssd, wout_ref[0:width, 0:width]) + mixsc_s[...]
    o_ref[0] = x + g1 * mix


def _mixer(x, mod, n1w, wa, wb, wdt, cw, cb, dtb, alog, dskip, ssdnw, sccw, scnw, wout, btri,
           expand, gsum):
    bsz, seq, d = x.shape
    tl = MIX_TILE
    xbc_w = cw.shape[1]
    consts = (n1w, wa, wb, wdt, cw, cb, dtb, alog, dskip, ssdnw, sccw, scnw, wout, btri, expand,
              gsum)
    return pl.pallas_call(
        _mixer_kernel,
        out_shape=jax.ShapeDtypeStruct((bsz, seq, d), F32),
        grid=(bsz, seq // tl),
        in_specs=[
            pl.BlockSpec((1, tl, d), lambda b, l: (b, l, 0)),
            pl.BlockSpec((1, mod.shape[1], d), lambda b, l: (b, 0, 0)),
        ] + [_resident(a.shape) for a in consts],
        out_specs=pl.BlockSpec((1, tl, d), lambda b, l: (b, l, 0)),
        scratch_shapes=[
            pltpu.VMEM((xbc_w // LANES, tl + HALO, LANES), F32),
            pltpu.VMEM((d // LANES, tl + HALO, LANES), F32),
            pltpu.VMEM((SSD_STATE, d), F32),
            pltpu.VMEM((tl, d), F32),
            pltpu.VMEM((tl, d), F32),
            pltpu.VMEM((xbc_w // LANES, tl, LANES), F32),
            pltpu.VMEM((d // LANES, tl, LANES), F32),
            pltpu.VMEM((tl, d), F32),
            pltpu.VMEM((tl, d), F32),
            pltpu.VMEM((tl, d), F32),
        ],
        compiler_params=pltpu.CompilerParams(
            dimension_semantics=("arbitrary", "arbitrary"),
            vmem_limit_bytes=VMEM_LIMIT_BYTES),
        name="mixer",
    )(x, mod, *consts)


def _ffn_kernel(x_ref, mod_ref, n2w_ref, wup_ref, fcw_ref, fcb_ref, wdown_ref, fnw_ref,
                o_ref, ubuf, act_s, dn_s):
    tl = x_ref.shape[1]
    d = x_ref.shape[2]
    hidden = wdown_ref.shape[0]
    n_col_chunks = hidden // FFN_COLS

    @pl.when(pl.program_id(1) == 0)
    def _():
        ubuf[:, 0:HALO, :] = jnp.zeros((ubuf.shape[0], HALO, LANES), F32)

    x = x_ref[0]
    sh2 = mod_ref[0, 3:4, :]
    sc2 = mod_ref[0, 4:5, :]
    g2 = mod_ref[0, 5:6, :]
    ms = jnp.mean(x * x, axis=-1, keepdims=True)
    h = (x * lax.rsqrt(ms + EPS)) * (n2w_ref[...] * (1.0 + sc2)) + sh2
    hb = h.astype(BF16)

    def up_conv(col0):
        up = _dot(hb, wup_ref[:, col0:col0 + FFN_COLS])
        cols = []
        for s, rows in enumerate(_lane_slabs(up)):
            slab = col0 // LANES + s
            ls = slice(slab * LANES, (slab + 1) * LANES)
            w = fcw_ref[:, ls]
            taps = [w[k:k + 1, :] for k in range(w.shape[0])]
            cols.append(jnp.concatenate(
                _causal_conv_slab(ubuf.at[slab], rows, taps, fcb_ref[:, ls], FFN_SUB), axis=0))
        return jnp.concatenate(cols, axis=1)

    for j in range(n_col_chunks):
        gate = up_conv(j * FFN_COLS)
        value = up_conv(hidden + j * FFN_COLS)
        act_s[:, j * FFN_COLS:(j + 1) * FFN_COLS] = (_silu(gate) * value).astype(BF16)

    sub = tl // FFN_SUB
    for i in range(FFN_SUB):
        rs = slice(i * sub, (i + 1) * sub)
        down = _dot(act_s[rs, :], wdown_ref[:, 0:d])
        for s, rows in enumerate(_lane_slabs(down)):
            dn_s[s, pl.ds(i * sub, sub // 2, stride=2), :] = rows[0:sub // 2, :]
            dn_s[s, pl.ds(i * sub + 1, sub // 2, stride=2), :] = rows[sub // 2:, :]
        x2 = x[rs, :] + g2 * jnp.concatenate([dn_s[s, rs, :] for s in range(d // LANES)], axis=1)
        ms2 = jnp.mean(x2 * x2, axis=-1, keepdims=True)
        o_ref[0, rs, :] = (x2 * lax.rsqrt(ms2 + EPS)) * fnw_ref[...]


def _ffn(x, mod, n2w, wup, fcw, fcb, wdown, fnw):
    bsz, seq, d = x.shape
    tl = FFN_TILE
    hidden = wdown.shape[0]
    consts = (n2w, wup, fcw, fcb, wdown, fnw)
    return pl.pallas_call(
        _ffn_kernel,
        out_shape=jax.ShapeDtypeStruct((bsz, seq, d), F32),
        grid=(bsz, seq // tl),
        in_specs=[
            pl.BlockSpec((1, tl, d), lambda b, l: (b, l, 0)),
            pl.BlockSpec((1, mod.shape[1], d), lambda b, l: (b, 0, 0)),
        ] + [_resident(a.shape) for a in consts],
        out_specs=pl.BlockSpec((1, tl, d), lambda b, l: (b, l, 0)),
        scratch_shapes=[
            pltpu.VMEM((2 * hidden // LANES, tl + HALO, LANES), F32),
            pltpu.VMEM((tl, hidden), BF16),
            pltpu.VMEM((d // LANES, tl, LANES), F32),
        ],
        compiler_params=pltpu.CompilerParams(
            dimension_semantics=("arbitrary", "arbitrary"),
            vmem_limit_bytes=VMEM_LIMIT_BYTES),
        name="ffn",
    )(x, mod, *consts)


def kernel(x, c, w_ada, b_ada, norm1_w, w_in, ssd_conv_w, ssd_conv_b, dt_bias, a_log, d_skip,
           ssd_norm_w, sc_conv_w, sc_norm_w, w_out, norm2_w, w_up, ffn_conv_w, ffn_conv_b, w_down,
           final_norm_w):
    bsz, seq, d = x.shape
    heads = dt_bias.shape[1]
    xbc_w = ssd_conv_w.shape[2]
    hidden = w_down.shape[1]
    assert w_in.shape[0] == 1, "one layer: the final norm is fused into the ffn call"
    assert d % LANES == 0 and seq % FFN_TILE == 0 and seq % MIX_TILE == 0
    assert heads * HEAD_DIM == d and heads <= 2 * SUBLANES and hidden % FFN_COLS == 0
    assert xbc_w == d + 2 * SSD_GROUPS * SSD_STATE

    t = jnp.arange(SCAN_BLOCK)
    btri = (t[:, None] >= t[None, :]).astype(BF16)
    ch = jnp.arange(d)
    j2 = jnp.arange(2 * LANES)
    expand = ((j2[:, None] % LANES) == (ch[None, :] // HEAD_DIM)).astype(BF16)
    gsum = ((ch[:, None] // SC_GROUP) == jnp.arange(LANES)[None, :]).astype(BF16)

    def pad_lanes(v):
        return jnp.pad(v, ((0, 0), (0, LANES - v.shape[1])))

    mod = _adaln_mod(c, w_ada[0], b_ada[0]).reshape(bsz, 6, d)
    def widen(w):
        return jnp.pad(w, ((0, 0), (0, LANES))) if w.shape[1] % (8 * LANES) == 0 else w

    wi = w_in[0]
    dt0 = d + xbc_w
    wa = wi[:, :dt0].astype(BF16)
    wb = widen(wi[:, dt0 + heads:]).astype(BF16)
    wdt = pad_lanes(wi[:, dt0:dt0 + heads]).astype(BF16)
    x1 = _mixer(
        x, mod, norm1_w, wa, wb, wdt, ssd_conv_w[0], ssd_conv_b, pad_lanes(dt_bias),
        pad_lanes(a_log),
        jnp.repeat(d_skip[0], HEAD_DIM)[None], ssd_norm_w, sc_conv_w[0], sc_norm_w,
        widen(w_out[0]).astype(BF16), btri, widen(expand), gsum)
    return _ffn(x1, mod, norm2_w, w_up[0].astype(BF16), ffn_conv_w[0], ffn_conv_b,
                widen(w_down[0]).astype(BF16), final_norm_w[None])
```
